```python
import jax, jax.numpy as jnp
from jax import lax
import numpy as np

D_MODEL = 1024
BATCH = 8
SEQ = 2048
DEPTH = 4

GDN_HEADS = 4
GDN_DK = 128
GDN_DV = 128
CONV_WIDTH = 4
GLA_HEADS = 4
GLA_DK = 128
GLA_DV = 256
GLA_GATE_RANK = 16
GLA_TAU = 16.0
CHUNK = 64
D_FF = 4 * D_MODEL
EPS = 1e-6

GDN_QK = GDN_HEADS * GDN_DK
GDN_V = GDN_HEADS * GDN_DV
GLA_QK = GLA_HEADS * GLA_DK
GLA_V = GLA_HEADS * GLA_DV
IN_SPLITS = (GDN_QK, GDN_QK, GDN_V, GDN_V, GDN_HEADS, GDN_HEADS,
             GLA_QK, GLA_QK, GLA_V, GLA_V, GLA_GATE_RANK,
             D_MODEL, D_MODEL)
IN_COLS = sum(IN_SPLITS)

kernel_name = 'hybrid_gdn_gla_sqrelu_sandwich'


def rms_norm(x, w):
    xf = x.astype(jnp.float32)
    y = xf * lax.rsqrt(jnp.mean(xf * xf, axis=-1, keepdims=True) + EPS)
    return (y * w.astype(jnp.float32)).astype(x.dtype)


def rms_norm_f32(x, w):
    return x * lax.rsqrt(jnp.mean(x * x, axis=-1, keepdims=True) + EPS) * w.astype(jnp.float32)


def l2_normalize(x):
    return x * lax.rsqrt(jnp.sum(x * x, axis=-1, keepdims=True) + EPS)


def causal_depthwise_conv(x, w):
    c = x.shape[-1]
    return lax.conv_general_dilated(
        x, w[:, None, :].astype(x.dtype), window_strides=(1,),
        padding=[(CONV_WIDTH - 1, 0)], dimension_numbers=('NWC', 'WIO', 'NWC'),
        feature_group_count=c)


def split_heads(t, nh, d):
    b, s, _ = t.shape
    return t.reshape(b, s, nh, d).transpose(0, 2, 1, 3).astype(jnp.float32)


def gated_delta_chunked(q, k, v, g, beta):
    bsz, nh, s, dk = q.shape
    dv = v.shape[-1]
    n = s // CHUNK
    q = q.reshape(bsz, nh, n, CHUNK, dk)
    k = k.reshape(bsz, nh, n, CHUNK, dk)
    v = v.reshape(bsz, nh, n, CHUNK, dv)
    beta = beta.reshape(bsz, nh, n, CHUNK)
    g = jnp.cumsum(g.reshape(bsz, nh, n, CHUNK), axis=-1)
    causal = jnp.tril(jnp.ones((CHUNK, CHUNK), dtype=bool))
    strict = jnp.tril(jnp.ones((CHUNK, CHUNK), dtype=bool), -1)
    decay = jnp.exp(jnp.where(causal, g[..., :, None] - g[..., None, :], -jnp.inf))
    k_beta = k * beta[..., None]
    a_mat = jnp.where(strict, jnp.einsum('bhnid,bhnjd->bhnij', k_beta, k) * decay, 0.0)
    t_mat = a_mat + jnp.eye(CHUNK, dtype=q.dtype)
    u = lax.linalg.triangular_solve(t_mat, v * beta[..., None], left_side=True,
                                    lower=True, unit_diagonal=True)
    w = lax.linalg.triangular_solve(t_mat, k_beta * jnp.exp(g)[..., None], left_side=True,
                                    lower=True, unit_diagonal=True)
    qk = jnp.where(causal, jnp.einsum('bhnid,bhnjd->bhnij', q, k) * decay, 0.0)
    q_dec = q * jnp.exp(g)[..., None]
    g_last = g[..., -1]
    k_dec = k * jnp.exp(g_last[..., None] - g)[..., None]
    chunk_decay = jnp.exp(g_last)

    def step(state, xs):
        u_c, w_c, qk_c, qd_c, kd_c, dec_c = xs
        v_new = u_c - jnp.einsum('bhcd,bhde->bhce', w_c, state)
        o_c = (jnp.einsum('bhcd,bhde->bhce', qd_c, state)
               + jnp.einsum('bhij,bhje->bhie', qk_c, v_new))
        state = state * dec_c[..., None, None] + jnp.einsum('bhcd,bhce->bhde', kd_c, v_new)
        return state, o_c

    xs = (jnp.moveaxis(u, 2, 0), jnp.moveaxis(w, 2, 0), jnp.moveaxis(qk, 2, 0),
          jnp.moveaxis(q_dec, 2, 0), jnp.moveaxis(k_dec, 2, 0), jnp.moveaxis(chunk_decay, 2, 0))
    state0 = jnp.zeros((bsz, nh, dk, dv), q.dtype)
    _, o = lax.scan(step, state0, xs)
    return jnp.moveaxis(o, 0, 2).reshape(bsz, nh, s, dv)


def gla_chunked(q, k, v, log_a):
    bsz, nh, s, dk = q.shape
    dv = v.shape[-1]
    n = s // CHUNK
    q = q.reshape(bsz, nh, n, CHUNK, dk)
    k = k.reshape(bsz, nh, n, CHUNK, dk)
    v = v.reshape(bsz, nh, n, CHUNK, dv)
    b = jnp.cumsum(log_a.reshape(bsz, nh, n, CHUNK, dk), axis=-2)
    b_last = b[..., -1:, :]
    b_ref = b[..., CHUNK // 2:CHUNK // 2 + 1, :]
    causal = jnp.tril(jnp.ones((CHUNK, CHUNK), dtype=bool))
    q_in = q * jnp.exp(b - b_ref)
    k_in = k * jnp.exp(b_ref - b)
    attn = jnp.where(causal, jnp.einsum('bhnid,bhnjd->bhnij', q_in, k_in), 0.0)
    o_intra = jnp.einsum('bhnij,bhnje->bhnie', attn, v)
    d_state = jnp.einsum('bhncd,bhnce->bhnde', k * jnp.exp(b_last - b), v)
    chunk_decay = jnp.exp(b_last[..., 0, :])

    def step(state, xs):
        ds_c, dec_c = xs
        return state * dec_c[..., None] + ds_c, state

    state0 = jnp.zeros((bsz, nh, dk, dv), q.dtype)
    _, s_prev = lax.scan(step, state0, (jnp.moveaxis(d_state, 2, 0), jnp.moveaxis(chunk_decay, 2, 0)))
    s_prev = jnp.moveaxis(s_prev, 0, 2)
    o_inter = jnp.einsum('bhncd,bhnde->bhnce', q * jnp.exp(b), s_prev)
    return (o_intra + o_inter).reshape(bsz, nh, s, dv)


def gdn_branch(q, k, v, z, b_logit, a_logit, conv_w, a_log, dt_bias, norm_w):
    bsz, s, _ = q.shape
    qkv = jax.nn.silu(causal_depthwise_conv(jnp.concatenate([q, k, v], axis=-1), conv_w))
    q, k, v = jnp.split(qkv, [GDN_QK, 2 * GDN_QK], axis=-1)
    q = l2_normalize(split_heads(q, GDN_HEADS, GDN_DK)) * (GDN_DK ** -0.5)
    k = l2_normalize(split_heads(k, GDN_HEADS, GDN_DK))
    v = split_heads(v, GDN_HEADS, GDN_DV)
    beta = jax.nn.sigmoid(b_logit.astype(jnp.float32)).transpose(0, 2, 1)
    g = -(jnp.exp(a_log.astype(jnp.float32))
          * jax.nn.softplus(a_logit.astype(jnp.float32) + dt_bias.astype(jnp.float32)))
    g = g.transpose(0, 2, 1)
    o = gated_delta_chunked(q, k, v, g, beta).transpose(0, 2, 1, 3)
    zg = jax.nn.silu(z.reshape(bsz, s, GDN_HEADS, GDN_DV).astype(jnp.float32))
    o = rms_norm_f32(o, norm_w) * zg
    return o.reshape(bsz, s, GDN_V).astype(q.dtype)


def gla_branch(q, k, v, r, gate_lr, gate_w2, gate_b, norm_w):
    bsz, s, _ = q.shape
    qh = split_heads(q, GLA_HEADS, GLA_DK) * (GLA_DK ** -0.5)
    kh = split_heads(k, GLA_HEADS, GLA_DK)
    vh = split_heads(v, GLA_HEADS, GLA_DV)
    gate_logit = (gate_lr @ gate_w2.astype(gate_lr.dtype)).astype(jnp.float32) + gate_b.astype(jnp.float32)
    log_a = jax.nn.log_sigmoid(gate_logit) / GLA_TAU
    log_a = split_heads(log_a, GLA_HEADS, GLA_DK)
    o = gla_chunked(qh, kh, vh, log_a).transpose(0, 2, 1, 3)
    rg = jax.nn.silu(r.reshape(bsz, s, GLA_HEADS, GLA_DV).astype(jnp.float32))
    o = rms_norm_f32(o, norm_w) * rg
    return o.reshape(bsz, s, GLA_V).astype(q.dtype)


def setup_inputs(seed: int = 0) -> dict:
    key = jax.random.key(seed)
    ks = jax.random.split(key, 20)
    f32 = jnp.float32
    L, D = DEPTH, D_MODEL

    def normal(k, shape, scale):
        return jax.random.normal(k, shape, f32) * scale

    def gain(k, shape):
        return 1.0 + 0.02 * jax.random.normal(k, shape, f32)

    x = jax.random.normal(ks[0], (BATCH, SEQ, D), f32)
    w_in = normal(ks[1], (L, D, IN_COLS), D ** -0.5)
    conv_w = normal(ks[2], (L, CONV_WIDTH, 2 * GDN_QK + GDN_V), CONV_WIDTH ** -0.5)
    a_log = jnp.log(jax.random.uniform(ks[3], (L, GDN_HEADS), f32, 1.0, 16.0))
    dt = jnp.exp(jax.random.uniform(ks[4], (L, GDN_HEADS), f32, np.log(1e-3), np.log(1e-1)))
    dt_bias = dt + jnp.log(-jnp.expm1(-dt))
    gdn_norm = gain(ks[5], (L, GDN_DV))
    gla_gate_w2 = normal(ks[6], (L, GLA_GATE_RANK, GLA_QK), GLA_GATE_RANK ** -0.5)
    gla_gate_b = normal(ks[7], (L, GLA_QK), 0.01)
    gla_norm = gain(ks[8], (L, GLA_DV))
    w_out_a = normal(ks[9], (L, GDN_V, D), GDN_V ** -0.5)
    w_out_b = normal(ks[10], (L, GLA_V, D), GLA_V ** -0.5)
    w_o = normal(ks[11], (L, D, D), D ** -0.5)
    norm_mix_pre = gain(ks[12], (L, D))
    norm_mix_post = gain(ks[13], (L, D))
    norm_mlp_pre = gain(ks[14], (L, D))
    norm_mlp_post = gain(ks[15], (L, D))
    w_mlp_up = normal(ks[16], (L, D, D_FF), D ** -0.5)
    w_mlp_down = normal(ks[17], (L, D_FF, D), D_FF ** -0.5)
    return {'x': x, 'w_in': w_in, 'conv_w': conv_w, 'a_log': a_log, 'dt_bias': dt_bias,
            'gdn_norm': gdn_norm, 'gla_gate_w2': gla_gate_w2, 'gla_gate_b': gla_gate_b,
            'gla_norm': gla_norm, 'w_out_a': w_out_a, 'w_out_b': w_out_b, 'w_o': w_o,
            'norm_mix_pre': norm_mix_pre, 'norm_mix_post': norm_mix_post,
            'norm_mlp_pre': norm_mlp_pre, 'norm_mlp_post': norm_mlp_post,
            'w_mlp_up': w_mlp_up, 'w_mlp_down': w_mlp_down}


def reference(x, w_in, conv_w, a_log, dt_bias, gdn_norm, gla_gate_w2, gla_gate_b, gla_norm,
              w_out_a, w_out_b, w_o, norm_mix_pre, norm_mix_post, norm_mlp_pre, norm_mlp_post,
              w_mlp_up, w_mlp_down):
    split_idx = list(np.cumsum(IN_SPLITS)[:-1])
    for l in range(DEPTH):
        h = rms_norm(x, norm_mix_pre[l])
        proj = h @ w_in[l]
        (a_q, a_k, a_v, a_z, a_b, a_a,
         b_q, b_k, b_v, b_r, b_glr,
         gate_a, gate_b) = jnp.split(proj, split_idx, axis=-1)
        y_a = gdn_branch(a_q, a_k, a_v, a_z, a_b, a_a, conv_w[l], a_log[l], dt_bias[l], gdn_norm[l])
        y_b = gla_branch(b_q, b_k, b_v, b_r, b_glr, gla_gate_w2[l], gla_gate_b[l], gla_norm[l])
        y_a = y_a @ w_out_a[l]
        y_b = y_b @ w_out_b[l]
        merged = jax.nn.sigmoid(gate_a) * y_a + jax.nn.sigmoid(gate_b) * y_b
        x = x + rms_norm(merged @ w_o[l], norm_mix_post[l])
        h = rms_norm(x, norm_mlp_pre[l])
        u = jnp.square(jax.nn.relu(h @ w_mlp_up[l]))
        x = x + rms_norm(u @ w_mlp_down[l], norm_mlp_post[l])
    return x
```

```python
import functools

import jax
import jax.numpy as jnp
from jax import lax
from jax.experimental import pallas as pl
from jax.experimental.pallas import tpu as pltpu

D_MODEL = 1024
DEPTH = 4
GDN_HEADS = 4
GDN_DK = 128
GDN_DV = 128
CONV_WIDTH = 4
GLA_HEADS = 4
GLA_DK = 128
GLA_DV = 256
GLA_GATE_RANK = 16
GLA_TAU = 16.0
CHUNK = 64
D_FF = 4 * D_MODEL
EPS = 1e-6

GDN_QK = GDN_HEADS * GDN_DK
GDN_V = GDN_HEADS * GDN_DV
GLA_QK = GLA_HEADS * GLA_DK
GLA_V = GLA_HEADS * GLA_DV

LANES = 128
CARRY_ROWS = 8

COL_QKV_A = 0
COL_Z = COL_QKV_A + 2 * GDN_QK + GDN_V
COL_QB = COL_Z + GDN_V
COL_KB = COL_QB + GLA_QK
COL_VB = COL_KB + GLA_QK
COL_R = COL_VB + GLA_V
COL_GA = COL_R + GLA_V
COL_GB = COL_GA + D_MODEL
COL_SMALL = COL_GB + D_MODEL
PROJ_COLS = COL_SMALL + LANES
SM_BETA = 0
SM_DECAY = GDN_HEADS
SM_GLR = 2 * GDN_HEADS

VMEM_LIMIT = 56 * 1024 * 1024

F32 = jnp.float32
BF16 = jnp.bfloat16
HI = lax.Precision.HIGHEST


def _dot(a, b, precision=None):
    return jnp.dot(a, b, preferred_element_type=F32, precision=precision)


def _dot_nt(a, b, precision=None):
    return lax.dot_general(a, b, (((1,), (1,)), ((), ())),
                           preferred_element_type=F32, precision=precision)


def _dot_tn(a, b, precision=None):
    return lax.dot_general(a, b, (((0,), (0,)), ((), ())),
                           preferred_element_type=F32, precision=precision)


def _rms(x, w):
    return x * lax.rsqrt(jnp.mean(x * x, axis=-1, keepdims=True) + EPS) * w


def _sigmoid(x):
    return 1.0 / (1.0 + jnp.exp(-x))


def _silu(x):
    return x * _sigmoid(x)


def _softplus(x):
    return jnp.maximum(x, 0.0) + jnp.log(1.0 + jnp.exp(-jnp.abs(x)))


def _const_spec(shape):
    nd = len(shape)
    return pl.BlockSpec(shape, lambda *_: (0,) * nd, pipeline_mode=pl.Buffered(1))


def _inproj_kernel(x_ref, nw_ref, w_ref, o_ref):
    h = _rms(x_ref[...], nw_ref[...]).astype(BF16)
    o_ref[...] = _dot(h, w_ref[...])


def _inproj(x2, nw, w, tm=256):
    t = x2.shape[0]
    return pl.pallas_call(
        _inproj_kernel,
        grid=(t // tm,),
        in_specs=[pl.BlockSpec((tm, D_MODEL), lambda i: (i, 0)),
                  _const_spec((1, D_MODEL)),
                  _const_spec((D_MODEL, PROJ_COLS))],
        out_specs=pl.BlockSpec((tm, PROJ_COLS), lambda i: (i, 0)),
        out_shape=jax.ShapeDtypeStruct((t, PROJ_COLS), F32),
        compiler_params=pltpu.CompilerParams(
            dimension_semantics=("arbitrary",), vmem_limit_bytes=VMEM_LIMIT),
        name="inproj",
    )(x2, nw, w)


def _tri_masks():
    row = lax.broadcasted_iota(jnp.int32, (CHUNK, CHUNK), 0)
    col = lax.broadcasted_iota(jnp.int32, (CHUNK, CHUNK), 1)
    return row >= col, row > col, row == col


def _unit_lower_inverse(a_mat, eye):
    x = eye - a_mat
    y = _dot(a_mat, a_mat, HI)
    power = 2
    while True:
        x = x + _dot(x, y, HI)
        power *= 2
        if power >= CHUNK:
            break
        y = _dot(y, y, HI)
    return x


def _gdn_kernel(qkv_ref, z_ref, sm_ref, cw_ref, alog_ref, dt_ref, nw_ref, o_ref,
                state_ref, carry_ref, act_ref, beta_ref, g_ref, *, ts):
    @pl.when(pl.program_id(1) == 0)
    def _():
        state_ref[...] = jnp.zeros_like(state_ref)
        carry_ref[...] = jnp.zeros_like(carry_ref)

    n_groups = (2 * GDN_QK + GDN_V) // LANES
    for j in range(n_groups):
        cs = slice(j * LANES, (j + 1) * LANES)
        x = qkv_ref[:, cs]
        xx = jnp.concatenate([carry_ref[:, cs], x], axis=0)
        base = CARRY_ROWS - (CONV_WIDTH - 1)
        acc = xx[base:base + ts] * cw_ref[0:1, cs]
        for k in range(1, CONV_WIDTH):
            acc = acc + xx[base + k:base + k + ts] * cw_ref[k:k + 1, cs]
        carry_ref[:, cs] = x[ts - CARRY_ROWS:ts]
        a = _silu(acc)
        if j < 2 * GDN_HEADS:
            a = a * lax.rsqrt(jnp.sum(a * a, axis=-1, keepdims=True) + EPS)
            if j < GDN_HEADS:
                a = a * (GDN_DK ** -0.5)
        act_ref[:, cs] = a

    sm = sm_ref[...]
    beta_ref[...] = _sigmoid(sm)
    g_ref[...] = -(jnp.exp(alog_ref[...]) * _softplus(sm + dt_ref[...]))

    causal, strict, diag = _tri_masks()
    eye = jnp.where(diag, 1.0, 0.0).astype(F32)
    tril = jnp.where(causal, 1.0, 0.0).astype(F32)
    rrow = lax.broadcasted_iota(jnp.int32, (CHUNK, 4 * CHUNK), 0)
    rcol = lax.broadcasted_iota(jnp.int32, (CHUNK, 4 * CHUNK), 1)
    rhs_mask = ((rcol < CHUNK) & (rrow > rcol)) | (rcol >= 2 * CHUNK)
    nw = nw_ref[...]

    def chunk_body(c, carry):
        rows = pl.ds(pl.multiple_of(c * CHUNK, CHUNK), CHUNK)
        beta_c = beta_ref[rows, :]
        g_c = g_ref[rows, :]
        for h in range(GDN_HEADS):
            q = act_ref[rows, h * GDN_DK:(h + 1) * GDN_DK]
            k = act_ref[rows, GDN_QK + h * GDN_DK:GDN_QK + (h + 1) * GDN_DK]
            v = act_ref[rows, 2 * GDN_QK + h * GDN_DV:2 * GDN_QK + (h + 1) * GDN_DV]
            beta = beta_c[:, SM_BETA + h:SM_BETA + h + 1]
            graw = g_c[:, SM_DECAY + h:SM_DECAY + h + 1]
            dg = _dot(tril, jnp.where(rhs_mask, graw, 0.0), HI)
            decay = jnp.where(causal, jnp.exp(dg[:, :CHUNK]), 0.0)
            gc = dg[:, 2 * CHUNK:]
            gl = gc[CHUNK - 1:CHUNK, :]
            egc = jnp.exp(gc)
            kb = k * beta
            a_mat = jnp.where(strict, _dot_nt(kb, k, HI) * decay, 0.0)
            tinv = _unit_lower_inverse(a_mat, eye)
            uw = _dot(tinv, jnp.concatenate([v * beta, kb * egc], axis=1), HI)
            u = uw[:, :GDN_DV]
            w = uw[:, GDN_DV:]
            qk = _dot_nt(q, k, HI) * decay
            s_prev = state_ref[h]
            v_new = u - _dot(w, s_prev, HI)
            o = _dot(q * egc, s_prev, HI) + _dot(qk, v_new, HI)
            state_ref[h] = s_prev * jnp.exp(gl) + _dot_tn(k * jnp.exp(gl - gc), v_new, HI)
            zg = _silu(z_ref[rows, h * GDN_DV:(h + 1) * GDN_DV])
            o_ref[rows, h * GDN_DV:(h + 1) * GDN_DV] = _rms(o, nw) * zg
        return carry

    lax.fori_loop(0, ts // CHUNK, chunk_body, 0)


def _gdn(proj, cw, alog_row, dt_row, nw, batch, seq, ts=512):
    ns = seq // ts
    qkv_cols = 2 * GDN_QK + GDN_V
    return pl.pallas_call(
        functools.partial(_gdn_kernel, ts=ts),
        grid=(batch, ns),
        in_specs=[pl.BlockSpec((ts, qkv_cols), lambda b, s: (b * ns + s, COL_QKV_A // qkv_cols)),
                  pl.BlockSpec((ts, GDN_V), lambda b, s: (b * ns + s, COL_Z // GDN_V)),
                  pl.BlockSpec((ts, LANES), lambda b, s: (b * ns + s, COL_SMALL // LANES)),
                  _const_spec((CONV_WIDTH, qkv_cols)),
                  _const_spec((1, LANES)),
                  _const_spec((1, LANES)),
                  _const_spec((1, GDN_DV))],
        out_specs=pl.BlockSpec((ts, GDN_V), lambda b, s: (b * ns + s, 0)),
        out_shape=jax.ShapeDtypeStruct((batch * seq, GDN_V), F32),
        scratch_shapes=[pltpu.VMEM((GDN_HEADS, GDN_DK, GDN_DV), F32),
                        pltpu.VMEM((CARRY_ROWS, qkv_cols), F32),
                        pltpu.VMEM((ts, qkv_cols), F32),
                        pltpu.VMEM((ts, LANES), F32),
                        pltpu.VMEM((ts, LANES), F32)],
        compiler_params=pltpu.CompilerParams(
            dimension_semantics=("arbitrary", "arbitrary"), vmem_limit_bytes=VMEM_LIMIT),
        name="gdn_mixer",
    )(proj, proj, proj, cw, alog_row, dt_row, nw)


def _gla_kernel(q_ref, k_ref, v_ref, r_ref, sm_ref, w2_ref, gb_ref, nw_ref, o_ref,
                state_ref, la_ref, *, ts):
    @pl.when(pl.program_id(1) == 0)
    def _():
        state_ref[...] = jnp.zeros_like(state_ref)

    logit = _dot(sm_ref[...], w2_ref[...], HI) + gb_ref[...]
    la_ref[...] = -_softplus(-logit) * (1.0 / GLA_TAU)

    causal, _, _ = _tri_masks()
    tril = jnp.where(causal, 1.0, 0.0).astype(F32)
    nw = nw_ref[...]

    def chunk_body(c, carry):
        rows = pl.ds(pl.multiple_of(c * CHUNK, CHUNK), CHUNK)
        for h in range(GLA_HEADS):
            kc = slice(h * GLA_DK, (h + 1) * GLA_DK)
            vc = slice(h * GLA_DV, (h + 1) * GLA_DV)
            q = q_ref[rows, kc] * (GLA_DK ** -0.5)
            k = k_ref[rows, kc]
            v = v_ref[rows, vc]
            b = _dot(tril, la_ref[rows, kc], HI)
            b_last = b[CHUNK - 1:CHUNK, :]
            b_ref = b[CHUNK // 2:CHUNK // 2 + 1, :]
            q_in = q * jnp.exp(b - b_ref)
            k_in = k * jnp.exp(b_ref - b)
            attn = jnp.where(causal, _dot_nt(q_in, k_in, HI), 0.0)
            s_prev = state_ref[h]
            o = _dot(attn, v, HI) + _dot_nt(q * jnp.exp(b), s_prev, HI)
            state_ref[h] = s_prev * jnp.exp(b_last) + _dot_tn(v, k * jnp.exp(b_last - b), HI)
            rg = _silu(r_ref[rows, vc])
            o_ref[rows, vc] = _rms(o, nw) * rg
        return carry

    lax.fori_loop(0, ts // CHUNK, chunk_body, 0)


def _gla(proj, w2pad, gate_b, nw, batch, seq, ts=512):
    ns = seq // ts
    return pl.pallas_call(
        functools.partial(_gla_kernel, ts=ts),
        grid=(batch, ns),
        in_specs=[pl.BlockSpec((ts, GLA_QK), lambda b, s: (b * ns + s, COL_QB // GLA_QK)),
                  pl.BlockSpec((ts, GLA_QK), lambda b, s: (b * ns + s, COL_KB // GLA_QK)),
                  pl.BlockSpec((ts, GLA_V), lambda b, s: (b * ns + s, COL_VB // GLA_V)),
                  pl.BlockSpec((ts, GLA_V), lambda b, s: (b * ns + s, COL_R // GLA_V)),
                  pl.BlockSpec((ts, LANES), lambda b, s: (b * ns + s, COL_SMALL // LANES)),
                  _const_spec((LANES, GLA_QK)),
                  _const_spec((1, GLA_QK)),
                  _const_spec((1, GLA_DV))],
        out_specs=pl.BlockSpec((ts, GLA_V), lambda b, s: (b * ns + s, 0)),
        out_shape=jax.ShapeDtypeStruct((batch * seq, GLA_V), F32),
        scratch_shapes=[pltpu.VMEM((GLA_HEADS, GLA_DV, GLA_DK), F32),
                        pltpu.VMEM((ts, GLA_QK), F32)],
        compiler_params=pltpu.CompilerParams(
            dimension_semantics=("arbitrary", "arbitrary"), vmem_limit_bytes=VMEM_LIMIT),
        name="gla_mixer",
    )(proj, proj, proj, proj, proj, w2pad, gate_b, nw)


def _merge_mlp_kernel(ya_ref, yb_ref, ga_ref, gb_ref, x_ref, woa_ref, wob_ref, wo_ref,
                      n_post_ref, n_pre_ref, wup_ref, wdn_ref, n_mlp_ref, o_ref):
    ya = _dot(ya_ref[...].astype(BF16), woa_ref[...])
    yb = _dot(yb_ref[...].astype(BF16), wob_ref[...])
    merged = _sigmoid(ga_ref[...]) * ya + _sigmoid(gb_ref[...]) * yb
    y = _dot(merged.astype(BF16), wo_ref[...])
    x1 = x_ref[...] + _rms(y, n_post_ref[...])
    h = _rms(x1, n_pre_ref[...]).astype(BF16)
    acc = jnp.zeros(x1.shape, F32)
    for c in range(D_FF // D_MODEL):
        cs = slice(c * D_MODEL, (c + 1) * D_MODEL)
        u = jnp.square(jnp.maximum(_dot(h, wup_ref[:, cs]), 0.0))
        acc = acc + _dot(u.astype(BF16), wdn_ref[cs, :])
    o_ref[...] = x1 + _rms(acc, n_mlp_ref[...])


def _merge_mlp(ya, yb, proj, x2, woa, wob, wo, n_post, n_pre, wup, wdn, n_mlp, tm=256):
    t = x2.shape[0]
    row = lambda i: (i, 0)
    return pl.pallas_call(
        _merge_mlp_kernel,
        grid=(t // tm,),
        in_specs=[pl.BlockSpec((tm, GDN_V), row),
                  pl.BlockSpec((tm, GLA_V), row),
                  pl.BlockSpec((tm, D_MODEL), lambda i: (i, COL_GA // D_MODEL)),
                  pl.BlockSpec((tm, D_MODEL), lambda i: (i, COL_GB // D_MODEL)),
                  pl.BlockSpec((tm, D_MODEL), row),
                  _const_spec((GDN_V, D_MODEL)),
                  _const_spec((GLA_V, D_MODEL)),
                  _const_spec((D_MODEL, D_MODEL)),
                  _const_spec((1, D_MODEL)),
                  _const_spec((1, D_MODEL)),
                  _const_spec((D_MODEL, D_FF)),
                  _const_spec((D_FF, D_MODEL)),
                  _const_spec((1, D_MODEL))],
        out_specs=pl.BlockSpec((tm, D_MODEL), row),
        out_shape=jax.ShapeDtypeStruct((t, D_MODEL), F32),
        compiler_params=pltpu.CompilerParams(
            dimension_semantics=("arbitrary",), vmem_limit_bytes=VMEM_LIMIT),
        name="merge_mlp",
    )(ya, yb, proj, proj, x2, woa, wob, wo, n_post, n_pre, wup, wdn, n_mlp)


def _reorder_w_in(w):
    o = 0
    parts = {}
    for name, width in (("aq", GDN_QK), ("ak", GDN_QK), ("av", GDN_V), ("az", GDN_V),
                        ("ab", GDN_HEADS), ("aa", GDN_HEADS), ("bq", GLA_QK), ("bk", GLA_QK),
                        ("bv", GLA_V), ("br", GLA_V), ("bg", GLA_GATE_RANK),
                        ("ga", D_MODEL), ("gb", D_MODEL)):
        parts[name] = w[:, o:o + width]
        o += width
    small = jnp.concatenate([parts["ab"], parts["aa"], parts["bg"]], axis=1)
    small = jnp.pad(small, ((0, 0), (0, LANES - small.shape[1])))
    out = jnp.concatenate([parts["aq"], parts["ak"], parts["av"], parts["az"], parts["bq"],
                           parts["bk"], parts["bv"], parts["br"], parts["ga"], parts["gb"],
                           small], axis=1)
    return out.astype(BF16)


def _pad_row(v, offset):
    return jnp.zeros((1, LANES), F32).at[0, offset:offset + v.shape[0]].set(v)


def kernel(x, w_in, conv_w, a_log, dt_bias, gdn_norm, gla_gate_w2, gla_gate_b, gla_norm,
           w_out_a, w_out_b, w_o, norm_mix_pre, norm_mix_post, norm_mlp_pre, norm_mlp_post,
           w_mlp_up, w_mlp_down):
    batch, seq, d = x.shape
    x2 = x.reshape(batch * seq, d)
    for l in range(DEPTH):
        proj = _inproj(x2, norm_mix_pre[l][None, :], _reorder_w_in(w_in[l]))
        ya = _gdn(proj, conv_w[l], _pad_row(a_log[l], SM_DECAY), _pad_row(dt_bias[l], SM_DECAY),
                  gdn_norm[l][None, :], batch, seq)
        w2pad = jnp.zeros((LANES, GLA_QK), F32).at[SM_GLR:SM_GLR + GLA_GATE_RANK].set(gla_gate_w2[l])
        yb = _gla(proj, w2pad, gla_gate_b[l][None, :], gla_norm[l][None, :], batch, seq)
        x2 = _merge_mlp(ya, yb, proj, x2,
                        w_out_a[l].astype(BF16), w_out_b[l].astype(BF16), w_o[l].astype(BF16),
                        norm_mix_post[l][None, :], norm_mlp_pre[l][None, :],
                        w_mlp_up[l].astype(BF16), w_mlp_down[l].astype(BF16),
                        norm_mlp_post[l][None, :])
    return x2.reshape(batch, seq, d)
```

```python
import functools

import jax
import jax.numpy as jnp
from jax import lax
from jax.experimental import pallas as pl
from jax.experimental.pallas import tpu as pltpu

D_MODEL = 1024
DEPTH = 4
GDN_HEADS = 4
GDN_DK = 128
GDN_DV = 128
CONV_WIDTH = 4
GLA_HEADS = 4
GLA_DK = 128
GLA_DV = 256
GLA_GATE_RANK = 16
GLA_TAU = 16.0
CHUNK = 64
D_FF = 4 * D_MODEL
EPS = 1e-6

GDN_QK = GDN_HEADS * GDN_DK
GDN_V = GDN_HEADS * GDN_DV
GLA_QK = GLA_HEADS * GLA_DK
GLA_V = GLA_HEADS * GLA_DV

LANES = 128
SUBLANES = 8
CARRY_ROWS = SUBLANES

COL_QKV_A = 0
COL_Z = COL_QKV_A + 2 * GDN_QK + GDN_V
COL_QB = COL_Z + GDN_V
COL_KB = COL_QB + GLA_QK
COL_VB = COL_KB + GLA_QK
COL_R = COL_VB + GLA_V
COL_GA = COL_R + GLA_V
COL_GB = COL_GA + D_MODEL
COL_SMALL = COL_GB + D_MODEL
PROJ_COLS = COL_SMALL + LANES
SM_BETA = 0
SM_DECAY = GDN_HEADS
SM_GLR = 2 * GDN_HEADS

VMEM_LIMIT = 56 * 1024 * 1024

F32 = jnp.float32
BF16 = jnp.bfloat16


def _dot(a, b):
    return jnp.dot(a, b, preferred_element_type=F32)


def _dot_nt(a, b):
    return lax.dot_general(a, b, (((1,), (1,)), ((), ())), preferred_element_type=F32)


def _bf(x):
    return x.astype(BF16)


def _split3(x):
    hi = _bf(x)
    r = x - hi.astype(F32)
    mid = _bf(r)
    lo = _bf(r - mid.astype(F32))
    return hi, mid, lo


def _dot_exact_lhs(a_bf, x):
    n = x.shape[1]
    y = _dot(a_bf, jnp.concatenate(_split3(x), axis=1))
    return (y[:, :n] + y[:, n:2 * n]) + y[:, 2 * n:]


def _dot_x3(a, b):
    a_hi = _bf(a)
    a_lo = _bf(a - a_hi.astype(F32))
    b_hi = _bf(b)
    b_lo = _bf(b - b_hi.astype(F32))
    return (_dot(a_hi, b_lo) + _dot(a_lo, b_hi)) + _dot(a_hi, b_hi)


def _rms(x, w):
    return x * lax.rsqrt(jnp.mean(x * x, axis=-1, keepdims=True) + EPS) * w


def _sigmoid(x):
    return 1.0 / (1.0 + jnp.exp(-x))


def _silu(x):
    return x * _sigmoid(x)


def _softplus(x):
    return jnp.maximum(x, 0.0) + jnp.log(1.0 + jnp.exp(-jnp.abs(x)))


def _const_spec(shape):
    nd = len(shape)
    return pl.BlockSpec(shape, lambda *_: (0,) * nd, pipeline_mode=pl.Buffered(1))


def _tri_masks():
    row = lax.broadcasted_iota(jnp.int32, (CHUNK, CHUNK), 0)
    col = lax.broadcasted_iota(jnp.int32, (CHUNK, CHUNK), 1)
    return row >= col, row > col


def _inproj_kernel(x_ref, nw_ref, w_ref, o_ref):
    h = _bf(_rms(x_ref[...], nw_ref[...]))
    o_ref[...] = _dot(h, w_ref[...])


def _inproj(x2, nw, w, tm=256):
    t = x2.shape[0]
    return pl.pallas_call(
        _inproj_kernel,
        grid=(t // tm,),
        in_specs=[pl.BlockSpec((tm, D_MODEL), lambda i: (i, 0)),
                  _const_spec((1, D_MODEL)),
                  _const_spec((D_MODEL, PROJ_COLS))],
        out_specs=pl.BlockSpec((tm, PROJ_COLS), lambda i: (i, 0)),
        out_shape=jax.ShapeDtypeStruct((t, PROJ_COLS), F32),
        compiler_params=pltpu.CompilerParams(
            dimension_semantics=("arbitrary",), vmem_limit_bytes=VMEM_LIMIT),
        name="inproj",
    )(x2, nw, w)


def _gdn_kernel(qkv_ref, z_ref, sm_ref, cw_ref, alog_ref, dt_ref, nw_ref, o_ref,
                state_ref, carry_ref, act_ref, beta_ref, g_ref,
                u_ref, w_ref, qd_ref, kdt_ref, qk_ref, egl_ref, *, ts, solve_group):
    n_chunks = ts // CHUNK

    @pl.when(pl.program_id(1) == 0)
    def _():
        state_ref[...] = jnp.zeros_like(state_ref)
        carry_ref[...] = jnp.zeros_like(carry_ref)

    n_groups = (2 * GDN_QK + GDN_V) // LANES
    for j in range(n_groups):
        cs = slice(j * LANES, (j + 1) * LANES)
        x = qkv_ref[:, cs]
        xx = jnp.concatenate([carry_ref[:, cs], x], axis=0)
        base = CARRY_ROWS - (CONV_WIDTH - 1)
        acc = xx[base:base + ts] * cw_ref[0:1, cs]
        for k in range(1, CONV_WIDTH):
            acc = acc + xx[base + k:base + k + ts] * cw_ref[k:k + 1, cs]
        carry_ref[:, cs] = x[ts - CARRY_ROWS:ts]
        a = _silu(acc)
        if j < 2 * GDN_HEADS:
            a = a * lax.rsqrt(jnp.sum(a * a, axis=-1, keepdims=True) + EPS)
            if j < GDN_HEADS:
                a = a * (GDN_DK ** -0.5)
        act_ref[:, cs] = a

    sm = sm_ref[...]
    beta_ref[...] = _sigmoid(sm)
    g_ref[...] = -(jnp.exp(alog_ref[...]) * _softplus(sm + dt_ref[...]))

    causal, strict = _tri_masks()
    tril = _bf(jnp.where(causal, 1.0, 0.0))
    rrow = lax.broadcasted_iota(jnp.int32, (CHUNK, 4 * CHUNK), 0)
    rcol = lax.broadcasted_iota(jnp.int32, (CHUNK, 4 * CHUNK), 1)
    rhs_mask = ((rcol < CHUNK) & (rrow > rcol)) | (rcol >= 2 * CHUNK)

    def solve_body(grp, carry):
        insts = [(ci, h) for ci in range(solve_group) for h in range(GDN_HEADS)]
        chunk_ids = [grp * solve_group + ci for ci in range(solve_group)]
        rows = [pl.ds(pl.multiple_of(c * CHUNK, CHUNK), CHUNK) for c in chunk_ids]
        beta_c = [beta_ref[r, :] for r in rows]
        g_c = [g_ref[r, :] for r in rows]
        hv = [slice(h * GDN_DV, (h + 1) * GDN_DV) for h in range(GDN_HEADS)]

        dg = [_dot_exact_lhs(tril, jnp.where(
            rhs_mask, g_c[ci][:, SM_DECAY + h:SM_DECAY + h + 1], 0.0)) for ci, h in insts]
        q = [act_ref[rows[ci], h * GDN_DK:(h + 1) * GDN_DK] for ci, h in insts]
        k = [act_ref[rows[ci], GDN_QK + h * GDN_DK:GDN_QK + (h + 1) * GDN_DK] for ci, h in insts]
        kb = [k[i] * beta_c[ci][:, SM_BETA + h:SM_BETA + h + 1] for i, (ci, h) in enumerate(insts)]
        kq = [_dot_nt(_bf(jnp.concatenate([kb[i], q[i]], axis=0)), _bf(k[i]))
              for i in range(len(insts))]
        decay = [jnp.where(causal, jnp.exp(d[:, :CHUNK]), 0.0) for d in dg]
        gc = [d[:, 2 * CHUNK:] for d in dg]
        egc = [jnp.exp(x) for x in gc]
        p = [_bf(jnp.where(strict, kq[i][:CHUNK] * decay[i], 0.0)) for i in range(len(insts))]
        r = [jnp.concatenate(
            [act_ref[rows[ci], 2 * GDN_QK + h * GDN_DV:2 * GDN_QK + (h + 1) * GDN_DV]
             * beta_c[ci][:, SM_BETA + h:SM_BETA + h + 1], kb[i] * egc[i]], axis=1)
             for i, (ci, h) in enumerate(insts)]
        ar = [_dot(p[i], _bf(r[i])) for i in range(len(insts))]
        r = [r[i] - ar[i] for i in range(len(insts))]
        power = 2
        while power < CHUNK:
            pp = [_dot(x, x) for x in p]
            p = [_bf(x) for x in pp]
            ar = [_dot(p[i], _bf(r[i])) for i in range(len(insts))]
            r = [r[i] + ar[i] for i in range(len(insts))]
            power *= 2
        for i, (ci, h) in enumerate(insts):
            idx = chunk_ids[ci] * GDN_HEADS + h
            gl = gc[i][CHUNK - 1:CHUNK, :]
            u_ref[rows[ci], hv[h]] = r[i][:, :GDN_DV]
            w_ref[rows[ci], hv[h]] = _bf(r[i][:, GDN_DV:])
            qd_ref[rows[ci], hv[h]] = _bf(q[i] * egc[i])
            kdt_ref[idx] = _bf((k[i] * jnp.exp(gl - gc[i])).T)
            qk_ref[h, rows[ci], :] = _bf(kq[i][CHUNK:] * decay[i])
            egl_ref[pl.ds(pl.multiple_of(idx * SUBLANES, SUBLANES), SUBLANES), :] = (
                jnp.broadcast_to(jnp.exp(gl), (SUBLANES, LANES)))
        return carry

    lax.fori_loop(0, n_chunks // solve_group, solve_body, 0)

    nw = nw_ref[...]
    states = [state_ref[h] for h in range(GDN_HEADS)]
    heads = range(GDN_HEADS)
    hvs = [slice(h * GDN_DV, (h + 1) * GDN_DV) for h in heads]
    for c in range(n_chunks):
        rows = slice(c * CHUNK, (c + 1) * CHUNK)
        s_bf = [_bf(states[h]) for h in heads]
        ws = [_dot(w_ref[rows, hvs[h]], s_bf[h]) for h in heads]
        v_new = [_bf(u_ref[rows, hvs[h]] - ws[h]) for h in heads]
        ds = [_dot(kdt_ref[c * GDN_HEADS + h], v_new[h]) for h in heads]
        states = [states[h] * egl_ref[(c * GDN_HEADS + h) * SUBLANES:
                                      (c * GDN_HEADS + h) * SUBLANES + 1, :] + ds[h]
                  for h in heads]
        o = [_dot(qd_ref[rows, hvs[h]], s_bf[h]) + _dot(qk_ref[h, rows, :], v_new[h])
             for h in heads]
        for h in heads:
            o_ref[rows, hvs[h]] = _rms(o[h], nw) * _silu(z_ref[rows, hvs[h]])
    for h in range(GDN_HEADS):
        state_ref[h] = states[h]


def _gdn(proj, cw, alog_row, dt_row, nw, batch, seq, ts=512, solve_group=4):
    ns = seq // ts
    n_chunks = ts // CHUNK
    qkv_cols = 2 * GDN_QK + GDN_V
    return pl.pallas_call(
        functools.partial(_gdn_kernel, ts=ts, solve_group=solve_group),
        grid=(batch, ns),
        in_specs=[pl.BlockSpec((ts, qkv_cols), lambda b, s: (b * ns + s, COL_QKV_A // qkv_cols)),
                  pl.BlockSpec((ts, GDN_V), lambda b, s: (b * ns + s, COL_Z // GDN_V)),
                  pl.BlockSpec((ts, LANES), lambda b, s: (b * ns + s, COL_SMALL // LANES)),
                  _const_spec((CONV_WIDTH, qkv_cols)),
                  _const_spec((1, LANES)),
                  _const_spec((1, LANES)),
                  _const_spec((1, GDN_DV))],
        out_specs=pl.BlockSpec((ts, GDN_V), lambda b, s: (b * ns + s, 0)),
        out_shape=jax.ShapeDtypeStruct((batch * seq, GDN_V), F32),
        scratch_shapes=[pltpu.VMEM((GDN_HEADS, GDN_DK, GDN_DV), F32),
                        pltpu.VMEM((CARRY_ROWS, qkv_cols), F32),
                        pltpu.VMEM((ts, qkv_cols), F32),
                        pltpu.VMEM((ts, LANES), F32),
                        pltpu.VMEM((ts, LANES), F32),
                        pltpu.VMEM((ts, GDN_V), F32),
                        pltpu.VMEM((ts, GDN_V), BF16),
                        pltpu.VMEM((ts, GDN_QK), BF16),
                        pltpu.VMEM((n_chunks * GDN_HEADS, GDN_DK, CHUNK), BF16),
                        pltpu.VMEM((GDN_HEADS, ts, CHUNK), BF16),
                        pltpu.VMEM((n_chunks * GDN_HEADS * SUBLANES, LANES), F32)],
        compiler_params=pltpu.CompilerParams(
            dimension_semantics=("arbitrary", "arbitrary"), vmem_limit_bytes=VMEM_LIMIT),
        name="gdn_mixer",
    )(proj, proj, proj, cw, alog_row, dt_row, nw)


def _gla_kernel(q_ref, k_ref, v_ref, r_ref, sm_ref, w2_ref, gb_ref, nw_ref, o_ref,
                state_ref, la_ref, *, ts, group):
    @pl.when(pl.program_id(1) == 0)
    def _():
        state_ref[...] = jnp.zeros_like(state_ref)

    logit = _dot_x3(sm_ref[...], w2_ref[...]) + gb_ref[...]
    la_ref[...] = -_softplus(-logit) * (1.0 / GLA_TAU)

    causal, _ = _tri_masks()
    tril = _bf(jnp.where(causal, 1.0, 0.0))
    nw = nw_ref[...]

    def group_body(grp, carry):
        insts = [(ci, h) for ci in range(group) for h in range(GLA_HEADS)]
        n = len(insts)
        rows = [pl.ds(pl.multiple_of((grp * group + ci) * CHUNK, CHUNK), CHUNK)
                for ci in range(group)]
        kc = [slice(h * GLA_DK, (h + 1) * GLA_DK) for h in range(GLA_HEADS)]
        vc = [slice(h * GLA_DV, (h + 1) * GLA_DV) for h in range(GLA_HEADS)]
        b_all = [_dot_exact_lhs(tril, la_ref[r, :]) for r in rows]
        b = [b_all[ci][:, kc[h]] for ci, h in insts]
        q = [q_ref[rows[ci], kc[h]] * (GLA_DK ** -0.5) for ci, h in insts]
        k = [k_ref[rows[ci], kc[h]] for ci, h in insts]
        v = [v_ref[rows[ci], vc[h]] for ci, h in insts]
        b_last = [x[CHUNK - 1:CHUNK, :] for x in b]
        b_mid = [x[CHUNK // 2:CHUNK // 2 + 1, :] for x in b]
        attn = [_dot_nt(_bf(q[i] * jnp.exp(b[i] - b_mid[i])), _bf(k[i] * jnp.exp(b_mid[i] - b[i])))
                for i in range(n)]
        d_state = [_dot(_bf(v[i].T), _bf(k[i] * jnp.exp(b_last[i] - b[i]))) for i in range(n)]
        o_intra = [_dot(_bf(jnp.where(causal, attn[i], 0.0)), _bf(v[i])) for i in range(n)]
        states = [state_ref[h] for h in range(GLA_HEADS)]
        o_inter = [None] * n
        for i, (ci, h) in enumerate(insts):
            o_inter[i] = _dot_nt(_bf(q[i] * jnp.exp(b[i])), _bf(states[h]))
            states[h] = states[h] * jnp.exp(b_last[i]) + d_state[i]
        for h in range(GLA_HEADS):
            state_ref[h] = states[h]
        for i, (ci, h) in enumerate(insts):
            o_ref[rows[ci], vc[h]] = _rms(o_intra[i] + o_inter[i], nw) * _silu(r_ref[rows[ci], vc[h]])
        return carry

    lax.fori_loop(0, ts // (CHUNK * group), group_body, 0)


def _gla(proj, w2pad, gate_b, nw, batch, seq, ts=512, group=2):
    ns = seq // ts
    return pl.pallas_call(
        functools.partial(_gla_kernel, ts=ts, group=group),
        grid=(batch, ns),
        in_specs=[pl.BlockSpec((ts, GLA_QK), lambda b, s: (b * ns + s, COL_QB // GLA_QK)),
                  pl.BlockSpec((ts, GLA_QK), lambda b, s: (b * ns + s, COL_KB // GLA_QK)),
                  pl.BlockSpec((ts, GLA_V), lambda b, s: (b * ns + s, COL_VB // GLA_V)),
                  pl.BlockSpec((ts, GLA_V), lambda b, s: (b * ns + s, COL_R // GLA_V)),
                  pl.BlockSpec((ts, LANES), lambda b, s: (b * ns + s, COL_SMALL // LANES)),
                  _const_spec((LANES, GLA_QK)),
                  _const_spec((1, GLA_QK)),
                  _const_spec((1, GLA_DV))],
        out_specs=pl.BlockSpec((ts, GLA_V), lambda b, s: (b * ns + s, 0)),
        out_shape=jax.ShapeDtypeStruct((batch * seq, GLA_V), F32),
        scratch_shapes=[pltpu.VMEM((GLA_HEADS, GLA_DV, GLA_DK), F32),
                        pltpu.VMEM((ts, GLA_QK), F32)],
        compiler_params=pltpu.CompilerParams(
            dimension_semantics=("arbitrary", "arbitrary"), vmem_limit_bytes=VMEM_LIMIT),
        name="gla_mixer",
    )(proj, proj, proj, proj, proj, w2pad, gate_b, nw)


def _merge_mlp_kernel(ya_ref, yb_ref, ga_ref, gb_ref, x_ref, woa_ref, wob_ref, wo_ref,
                      n_post_ref, n_pre_ref, wup_ref, wdn_ref, n_mlp_ref, o_ref):
    ya = _dot(_bf(ya_ref[...]), woa_ref[...])
    yb = _dot(_bf(yb_ref[...]), wob_ref[...])
    merged = _sigmoid(ga_ref[...]) * ya + _sigmoid(gb_ref[...]) * yb
    y = _dot(_bf(merged), wo_ref[...])
    x1 = x_ref[...] + _rms(y, n_post_ref[...])
    h = _bf(_rms(x1, n_pre_ref[...]))
    acc = jnp.zeros(x1.shape, F32)
    for c in range(D_FF // D_MODEL):
        cs = slice(c * D_MODEL, (c + 1) * D_MODEL)
        u = jnp.square(jnp.maximum(_dot(h, wup_ref[:, cs]), 0.0))
        acc = acc + _dot(_bf(u), wdn_ref[cs, :])
    o_ref[...] = x1 + _rms(acc, n_mlp_ref[...])


def _merge_mlp(ya, yb, proj, x2, woa, wob, wo, n_post, n_pre, wup, wdn, n_mlp, tm=256):
    t = x2.shape[0]
    row = lambda i: (i, 0)
    return pl.pallas_call(
        _merge_mlp_kernel,
        grid=(t // tm,),
        in_specs=[pl.BlockSpec((tm, GDN_V), row),
                  pl.BlockSpec((tm, GLA_V), row),
                  pl.BlockSpec((tm, D_MODEL), lambda i: (i, COL_GA // D_MODEL)),
                  pl.BlockSpec((tm, D_MODEL), lambda i: (i, COL_GB // D_MODEL)),
                  pl.BlockSpec((tm, D_MODEL), row),
                  _const_spec((GDN_V, D_MODEL)),
                  _const_spec((GLA_V, D_MODEL)),
                  _const_spec((D_MODEL, D_MODEL)),
                  _const_spec((1, D_MODEL)),
                  _const_spec((1, D_MODEL)),
                  _const_spec((D_MODEL, D_FF)),
                  _const_spec((D_FF, D_MODEL)),
                  _const_spec((1, D_MODEL))],
        out_specs=pl.BlockSpec((tm, D_MODEL), row),
        out_shape=jax.ShapeDtypeStruct((t, D_MODEL), F32),
        compiler_params=pltpu.CompilerParams(
            dimension_semantics=("arbitrary",), vmem_limit_bytes=VMEM_LIMIT),
        name="merge_mlp",
    )(ya, yb, proj, proj, x2, woa, wob, wo, n_post, n_pre, wup, wdn, n_mlp)


def _reorder_w_in(w):
    o = 0
    parts = {}
    for name, width in (("aq", GDN_QK), ("ak", GDN_QK), ("av", GDN_V), ("az", GDN_V),
                        ("ab", GDN_HEADS), ("aa", GDN_HEADS), ("bq", GLA_QK), ("bk", GLA_QK),
                        ("bv", GLA_V), ("br", GLA_V), ("bg", GLA_GATE_RANK),
                        ("ga", D_MODEL), ("gb", D_MODEL)):
        parts[name] = w[:, o:o + width]
        o += width
    small = jnp.concatenate([parts["ab"], parts["aa"], parts["bg"]], axis=1)
    small = jnp.pad(small, ((0, 0), (0, LANES - small.shape[1])))
    out = jnp.concatenate([parts["aq"], parts["ak"], parts["av"], parts["az"], parts["bq"],
                           parts["bk"], parts["bv"], parts["br"], parts["ga"], parts["gb"],
                           small], axis=1)
    return out.astype(BF16)


def _pad_row(v, offset):
    return jnp.zeros((1, LANES), F32).at[0, offset:offset + v.shape[0]].set(v)


def kernel(x, w_in, conv_w, a_log, dt_bias, gdn_norm, gla_gate_w2, gla_gate_b, gla_norm,
           w_out_a, w_out_b, w_o, norm_mix_pre, norm_mix_post, norm_mlp_pre, norm_mlp_post,
           w_mlp_up, w_mlp_down):
    batch, seq, d = x.shape
    x2 = x.reshape(batch * seq, d)
    for l in range(DEPTH):
        proj = _inproj(x2, norm_mix_pre[l][None, :], _reorder_w_in(w_in[l]))
        ya = _gdn(proj, conv_w[l], _pad_row(a_log[l], SM_DECAY), _pad_row(dt_bias[l], SM_DECAY),
                  gdn_norm[l][None, :], batch, seq)
        w2pad = jnp.zeros((LANES, GLA_QK), F32).at[SM_GLR:SM_GLR + GLA_GATE_RANK].set(gla_gate_w2[l])
        yb = _gla(proj, w2pad, gla_gate_b[l][None, :], gla_norm[l][None, :], batch, seq)
        x2 = _merge_mlp(ya, yb, proj, x2,
                        w_out_a[l].astype(BF16), w_out_b[l].astype(BF16), w_o[l].astype(BF16),
                        norm_mix_post[l][None, :], norm_mlp_pre[l][None, :],
                        w_mlp_up[l].astype(BF16), w_mlp_down[l].astype(BF16),
                        norm_mlp_post[l][None, :])
    return x2.reshape(batch, seq, d)
```

```python
import functools

import jax
import jax.numpy as jnp
from jax import lax
from jax.experimental import pallas as pl
from jax.experimental.pallas import tpu as pltpu

D_MODEL = 1024
DEPTH = 4
GDN_HEADS = 4
GDN_DK = 128
GDN_DV = 128
CONV_WIDTH = 4
GLA_HEADS = 4
GLA_DK = 128
GLA_DV = 256
GLA_GATE_RANK = 16
GLA_TAU = 16.0
CHUNK = 64
D_FF = 4 * D_MODEL
EPS = 1e-6

GDN_QK = GDN_HEADS * GDN_DK
GDN_V = GDN_HEADS * GDN_DV
GLA_QK = GLA_HEADS * GLA_DK
GLA_V = GLA_HEADS * GLA_DV

LANES = 128
SUBLANES = 8
CARRY_ROWS = SUBLANES

COL_QKV_A = 0
COL_Z = COL_QKV_A + 2 * GDN_QK + GDN_V
COL_QB = COL_Z + GDN_V
COL_KB = COL_QB + GLA_QK
COL_VB = COL_KB + GLA_QK
COL_R = COL_VB + GLA_V
COL_GA = COL_R + GLA_V
COL_GB = COL_GA + D_MODEL
COL_SMALL = COL_GB + D_MODEL
PROJ_COLS = COL_SMALL + LANES
SM_BETA = 0
SM_DECAY = GDN_HEADS
SM_GLR = 2 * GDN_HEADS

VMEM_LIMIT = 56 * 1024 * 1024

F32 = jnp.float32
BF16 = jnp.bfloat16


def _dot(a, b):
    return jnp.dot(a, b, preferred_element_type=F32)


def _dot_nt(a, b):
    return lax.dot_general(a, b, (((1,), (1,)), ((), ())), preferred_element_type=F32)


def _bf(x):
    return x.astype(BF16)


def _split3(x):
    hi = _bf(x)
    r = x - hi.astype(F32)
    mid = _bf(r)
    lo = _bf(r - mid.astype(F32))
    return hi, mid, lo


def _dot_exact_lhs(a_bf, x):
    n = x.shape[1]
    y = _dot(a_bf, jnp.concatenate(_split3(x), axis=1))
    return (y[:, :n] + y[:, n:2 * n]) + y[:, 2 * n:]


def _dot_x3(a, b):
    a_hi = _bf(a)
    a_lo = _bf(a - a_hi.astype(F32))
    b_hi = _bf(b)
    b_lo = _bf(b - b_hi.astype(F32))
    return (_dot(a_hi, b_lo) + _dot(a_lo, b_hi)) + _dot(a_hi, b_hi)


def _rms(x, w):
    return x * lax.rsqrt(jnp.mean(x * x, axis=-1, keepdims=True) + EPS) * w


def _sigmoid(x):
    return 1.0 / (1.0 + jnp.exp(-x))


def _silu(x):
    return x * _sigmoid(x)


def _softplus(x):
    return jnp.maximum(x, 0.0) + jnp.log(1.0 + jnp.exp(-jnp.abs(x)))


def _const_spec(shape):
    nd = len(shape)
    return pl.BlockSpec(shape, lambda *_: (0,) * nd, pipeline_mode=pl.Buffered(1))


def _layer_spec(shape, layer):
    nd = len(shape)
    return pl.BlockSpec((None,) + tuple(shape), lambda *_: (layer,) + (0,) * nd,
                        pipeline_mode=pl.Buffered(1))


def _tri_masks():
    row = lax.broadcasted_iota(jnp.int32, (CHUNK, CHUNK), 0)
    col = lax.broadcasted_iota(jnp.int32, (CHUNK, CHUNK), 1)
    return row >= col, row > col


def _inproj_kernel(x_ref, nw_ref, w_ref, o_ref):
    h = _bf(_rms(x_ref[...], nw_ref[...]))
    o_ref[...] = _dot(h, w_ref[...])


def _inproj(x2, nw, w, layer, tm=256):
    t = x2.shape[0]
    return pl.pallas_call(
        _inproj_kernel,
        grid=(t // tm,),
        in_specs=[pl.BlockSpec((tm, D_MODEL), lambda i: (i, 0)),
                  _const_spec((1, D_MODEL)),
                  _layer_spec((D_MODEL, PROJ_COLS), layer)],
        out_specs=pl.BlockSpec((tm, PROJ_COLS), lambda i: (i, 0)),
        out_shape=jax.ShapeDtypeStruct((t, PROJ_COLS), F32),
        compiler_params=pltpu.CompilerParams(
            dimension_semantics=("arbitrary",), vmem_limit_bytes=VMEM_LIMIT),
        name="inproj",
    )(x2, nw, w)


def _gdn_kernel(qkv_ref, z_ref, sm_ref, cw_ref, alog_ref, dt_ref, nw_ref, o_ref,
                state_ref, carry_ref, act_ref, beta_ref, g_ref,
                tinv_ref, kg_ref, qd_ref, kdt_ref, qk_ref, egl_ref, *, ts, solve_group):
    n_chunks = ts // CHUNK

    @pl.when(pl.program_id(1) == 0)
    def _():
        state_ref[...] = jnp.zeros_like(state_ref)
        carry_ref[...] = jnp.zeros_like(carry_ref)

    n_groups = (2 * GDN_QK + GDN_V) // LANES
    for j in range(n_groups):
        cs = slice(j * LANES, (j + 1) * LANES)
        x = qkv_ref[:, cs]
        xx = jnp.concatenate([carry_ref[:, cs], x], axis=0)
        base = CARRY_ROWS - (CONV_WIDTH - 1)
        acc = xx[base:base + ts] * cw_ref[0:1, cs]
        for k in range(1, CONV_WIDTH):
            acc = acc + xx[base + k:base + k + ts] * cw_ref[k:k + 1, cs]
        carry_ref[:, cs] = x[ts - CARRY_ROWS:ts]
        a = _silu(acc)
        if j < 2 * GDN_HEADS:
            a = a * lax.rsqrt(jnp.sum(a * a, axis=-1, keepdims=True) + EPS)
            if j < GDN_HEADS:
                a = a * (GDN_DK ** -0.5)
        act_ref[:, cs] = a

    sm = sm_ref[...]
    beta_ref[...] = _sigmoid(sm)
    g_ref[...] = -(jnp.exp(alog_ref[...]) * _softplus(sm + dt_ref[...]))

    causal, strict = _tri_masks()
    tril = _bf(jnp.where(causal, 1.0, 0.0))
    eye = jnp.where(causal & jnp.logical_not(strict), 1.0, 0.0).astype(F32)
    rrow = lax.broadcasted_iota(jnp.int32, (CHUNK, 4 * CHUNK), 0)
    rcol = lax.broadcasted_iota(jnp.int32, (CHUNK, 4 * CHUNK), 1)
    rhs_mask = ((rcol < CHUNK) & (rrow > rcol)) | (rcol >= 2 * CHUNK)

    def solve_body(grp, carry):
        insts = [(ci, h) for ci in range(solve_group) for h in range(GDN_HEADS)]
        chunk_ids = [grp * solve_group + ci for ci in range(solve_group)]
        rows = [pl.ds(pl.multiple_of(c * CHUNK, CHUNK), CHUNK) for c in chunk_ids]
        beta_c = [beta_ref[r, :] for r in rows]
        g_c = [g_ref[r, :] for r in rows]
        hv = [slice(h * GDN_DV, (h + 1) * GDN_DV) for h in range(GDN_HEADS)]

        dg = [_dot_exact_lhs(tril, jnp.where(
            rhs_mask, g_c[ci][:, SM_DECAY + h:SM_DECAY + h + 1], 0.0)) for ci, h in insts]
        q = [act_ref[rows[ci], h * GDN_DK:(h + 1) * GDN_DK] for ci, h in insts]
        k = [act_ref[rows[ci], GDN_QK + h * GDN_DK:GDN_QK + (h + 1) * GDN_DK] for ci, h in insts]
        kb = [k[i] * beta_c[ci][:, SM_BETA + h:SM_BETA + h + 1] for i, (ci, h) in enumerate(insts)]
        kq = [_dot_nt(_bf(jnp.concatenate([kb[i], q[i]], axis=0)), _bf(k[i]))
              for i in range(len(insts))]
        decay = [jnp.where(causal, jnp.exp(d[:, :CHUNK]), 0.0) for d in dg]
        gc = [d[:, 2 * CHUNK:] for d in dg]
        egc = [jnp.exp(x) for x in gc]
        a_mat = [jnp.where(strict, kq[i][:CHUNK] * decay[i], 0.0) for i in range(len(insts))]
        p = [_bf(x) for x in a_mat]
        tinv = [eye - x for x in a_mat]
        power = 2
        while True:
            pp = [_dot(x, x) for x in p]
            p = [_bf(x) for x in pp]
            xp = [_dot(_bf(tinv[i]), p[i]) for i in range(len(insts))]
            tinv = [tinv[i] + xp[i] for i in range(len(insts))]
            power *= 2
            if power >= CHUNK:
                break
        for i, (ci, h) in enumerate(insts):
            idx = chunk_ids[ci] * GDN_HEADS + h
            gl = gc[i][CHUNK - 1:CHUNK, :]
            kg = k[i] * egc[i]
            kg_hi = _bf(kg)
            tinv_ref[idx] = _bf(tinv[i])
            kg_ref[idx, :CHUNK, :] = kg_hi
            kg_ref[idx, CHUNK:, :] = _bf(kg - kg_hi.astype(F32))
            qd_ref[rows[ci], hv[h]] = _bf(q[i] * egc[i])
            kdt_ref[idx] = _bf((k[i] * jnp.exp(gl - gc[i])).T)
            qk_ref[h, rows[ci], :] = _bf(kq[i][CHUNK:] * decay[i])
            egl_ref[pl.ds(pl.multiple_of(idx * SUBLANES, SUBLANES), SUBLANES), :] = (
                jnp.broadcast_to(jnp.exp(gl), (SUBLANES, LANES)))
        return carry

    lax.fori_loop(0, n_chunks // solve_group, solve_body, 0)

    nw = nw_ref[...]
    states = [state_ref[h] for h in range(GDN_HEADS)]
    heads = range(GDN_HEADS)
    hvs = [slice(h * GDN_DV, (h + 1) * GDN_DV) for h in heads]
    for c in range(n_chunks):
        rows = slice(c * CHUNK, (c + 1) * CHUNK)
        idx = [c * GDN_HEADS + h for h in heads]
        s_hi = [_bf(states[h]) for h in heads]
        s_lo = [_bf(states[h] - s_hi[h].astype(F32)) for h in heads]
        ks_a = [_dot(kg_ref[idx[h]], s_hi[h]) for h in heads]
        ks_b = [_dot(kg_ref[idx[h], :CHUNK, :], s_lo[h]) for h in heads]
        resid = [(act_ref[rows, 2 * GDN_QK + h * GDN_DV:2 * GDN_QK + (h + 1) * GDN_DV]
                  - ((ks_b[h] + ks_a[h][CHUNK:]) + ks_a[h][:CHUNK]))
                 * beta_ref[rows, SM_BETA + h:SM_BETA + h + 1] for h in heads]
        v_new = [_bf(_dot(tinv_ref[idx[h]], _bf(resid[h]))) for h in heads]
        ds = [_dot(kdt_ref[idx[h]], v_new[h]) for h in heads]
        states = [states[h] * egl_ref[idx[h] * SUBLANES:idx[h] * SUBLANES + 1, :] + ds[h]
                  for h in heads]
        o = [_dot(qd_ref[rows, hvs[h]], s_hi[h]) + _dot(qk_ref[h, rows, :], v_new[h])
             for h in heads]
        for h in heads:
            o_ref[rows, hvs[h]] = _rms(o[h], nw) * _silu(z_ref[rows, hvs[h]])
    for h in range(GDN_HEADS):
        state_ref[h] = states[h]


def _gdn(proj, cw, alog_row, dt_row, nw, batch, seq, ts=512, solve_group=4):
    ns = seq // ts
    n_chunks = ts // CHUNK
    qkv_cols = 2 * GDN_QK + GDN_V
    return pl.pallas_call(
        functools.partial(_gdn_kernel, ts=ts, solve_group=solve_group),
        grid=(batch, ns),
        in_specs=[pl.BlockSpec((ts, qkv_cols), lambda b, s: (b * ns + s, COL_QKV_A // qkv_cols)),
                  pl.BlockSpec((ts, GDN_V), lambda b, s: (b * ns + s, COL_Z // GDN_V)),
                  pl.BlockSpec((ts, LANES), lambda b, s: (b * ns + s, COL_SMALL // LANES)),
                  _const_spec((CONV_WIDTH, qkv_cols)),
                  _const_spec((1, LANES)),
                  _const_spec((1, LANES)),
                  _const_spec((1, GDN_DV))],
        out_specs=pl.BlockSpec((ts, GDN_V), lambda b, s: (b * ns + s, 0)),
        out_shape=jax.ShapeDtypeStruct((batch * seq, GDN_V), F32),
        scratch_shapes=[pltpu.VMEM((GDN_HEADS, GDN_DK, GDN_DV), F32),
                        pltpu.VMEM((CARRY_ROWS, qkv_cols), F32),
                        pltpu.VMEM((ts, qkv_cols), F32),
                        pltpu.VMEM((ts, LANES), F32),
                        pltpu.VMEM((ts, LANES), F32),
                        pltpu.VMEM((n_chunks * GDN_HEADS, CHUNK, CHUNK), BF16),
                        pltpu.VMEM((n_chunks * GDN_HEADS, 2 * CHUNK, GDN_DK), BF16),
                        pltpu.VMEM((ts, GDN_QK), BF16),
                        pltpu.VMEM((n_chunks * GDN_HEADS, GDN_DK, CHUNK), BF16),
                        pltpu.VMEM((GDN_HEADS, ts, CHUNK), BF16),
                        pltpu.VMEM((n_chunks * GDN_HEADS * SUBLANES, LANES), F32)],
        compiler_params=pltpu.CompilerParams(
            dimension_semantics=("arbitrary", "arbitrary"), vmem_limit_bytes=VMEM_LIMIT),
        name="gdn_mixer",
    )(proj, proj, proj, cw, alog_row, dt_row, nw)


def _gla_kernel(q_ref, k_ref, v_ref, r_ref, sm_ref, w2_ref, gb_ref, nw_ref, o_ref,
                state_ref, la_ref, *, ts, group):
    @pl.when(pl.program_id(1) == 0)
    def _():
        state_ref[...] = jnp.zeros_like(state_ref)

    logit = _dot_x3(sm_ref[...], w2_ref[...]) + gb_ref[...]
    la_ref[...] = -_softplus(-logit) * (1.0 / GLA_TAU)

    causal, _ = _tri_masks()
    tril = _bf(jnp.where(causal, 1.0, 0.0))
    nw = nw_ref[...]

    def group_body(grp, carry):
        insts = [(ci, h) for ci in range(group) for h in range(GLA_HEADS)]
        n = len(insts)
        rows = [pl.ds(pl.multiple_of((grp * group + ci) * CHUNK, CHUNK), CHUNK)
                for ci in range(group)]
        kc = [slice(h * GLA_DK, (h + 1) * GLA_DK) for h in range(GLA_HEADS)]
        vc = [slice(h * GLA_DV, (h + 1) * GLA_DV) for h in range(GLA_HEADS)]
        b_all = [_dot_exact_lhs(tril, la_ref[r, :]) for r in rows]
        b = [b_all[ci][:, kc[h]] for ci, h in insts]
        q = [q_ref[rows[ci], kc[h]] * (GLA_DK ** -0.5) for ci, h in insts]
        k = [k_ref[rows[ci], kc[h]] for ci, h in insts]
        v = [v_ref[rows[ci], vc[h]] for ci, h in insts]
        b_last = [x[CHUNK - 1:CHUNK, :] for x in b]
        b_mid = [x[CHUNK // 2:CHUNK // 2 + 1, :] for x in b]
        attn = [_dot_nt(_bf(q[i] * jnp.exp(b[i] - b_mid[i])), _bf(k[i] * jnp.exp(b_mid[i] - b[i])))
                for i in range(n)]
        d_state = [_dot(_bf(v[i].T), _bf(k[i] * jnp.exp(b_last[i] - b[i]))) for i in range(n)]
        o_intra = [_dot(_bf(jnp.where(causal, attn[i], 0.0)), _bf(v[i])) for i in range(n)]
        states = [state_ref[h] for h in range(GLA_HEADS)]
        o_inter = [None] * n
        for i, (ci, h) in enumerate(insts):
            o_inter[i] = _dot_nt(_bf(q[i] * jnp.exp(b[i])), _bf(states[h]))
            states[h] = states[h] * jnp.exp(b_last[i]) + d_state[i]
        for h in range(GLA_HEADS):
            state_ref[h] = states[h]
        for i, (ci, h) in enumerate(insts):
            o_ref[rows[ci], vc[h]] = _rms(o_intra[i] + o_inter[i], nw) * _silu(r_ref[rows[ci], vc[h]])
        return carry

    lax.fori_loop(0, ts // (CHUNK * group), group_body, 0)


def _gla(proj, w2pad, gate_b, nw, batch, seq, ts=512, group=2):
    ns = seq // ts
    return pl.pallas_call(
        functools.partial(_gla_kernel, ts=ts, group=group),
        grid=(batch, ns),
        in_specs=[pl.BlockSpec((ts, GLA_QK), lambda b, s: (b * ns + s, COL_QB // GLA_QK)),
                  pl.BlockSpec((ts, GLA_QK), lambda b, s: (b * ns + s, COL_KB // GLA_QK)),
                  pl.BlockSpec((ts, GLA_V), lambda b, s: (b * ns + s, COL_VB // GLA_V)),
                  pl.BlockSpec((ts, GLA_V), lambda b, s: (b * ns + s, COL_R // GLA_V)),
                  pl.BlockSpec((ts, LANES), lambda b, s: (b * ns + s, COL_SMALL // LANES)),
                  _const_spec((LANES, GLA_QK)),
                  _const_spec((1, GLA_QK)),
                  _const_spec((1, GLA_DV))],
        out_specs=pl.BlockSpec((ts, GLA_V), lambda b, s: (b * ns + s, 0)),
        out_shape=jax.ShapeDtypeStruct((batch * seq, GLA_V), F32),
        scratch_shapes=[pltpu.VMEM((GLA_HEADS, GLA_DV, GLA_DK), F32),
                        pltpu.VMEM((ts, GLA_QK), F32)],
        compiler_params=pltpu.CompilerParams(
            dimension_semantics=("arbitrary", "arbitrary"), vmem_limit_bytes=VMEM_LIMIT),
        name="gla_mixer",
    )(proj, proj, proj, proj, proj, w2pad, gate_b, nw)


def _merge_mlp_kernel(ya_ref, yb_ref, ga_ref, gb_ref, x_ref, woa_ref, wob_ref, wo_ref,
                      n_post_ref, n_pre_ref, wup_ref, wdn_ref, n_mlp_ref, o_ref):
    ya = _dot(_bf(ya_ref[...]), woa_ref[...])
    yb = _dot(_bf(yb_ref[...]), wob_ref[...])
    merged = _sigmoid(ga_ref[...]) * ya + _sigmoid(gb_ref[...]) * yb
    y = _dot(_bf(merged), wo_ref[...])
    x1 = x_ref[...] + _rms(y, n_post_ref[...])
    h = _bf(_rms(x1, n_pre_ref[...]))
    acc = jnp.zeros(x1.shape, F32)
    for c in range(D_FF // D_MODEL):
        cs = slice(c * D_MODEL, (c + 1) * D_MODEL)
        u = jnp.square(jnp.maximum(_dot(h, wup_ref[:, cs]), 0.0))
        acc = acc + _dot(_bf(u), wdn_ref[cs, :])
    o_ref[...] = x1 + _rms(acc, n_mlp_ref[...])


def _merge_mlp(ya, yb, proj, x2, woa, wob, wo, n_post, n_pre, wup, wdn, n_mlp, layer, tm=256):
    t = x2.shape[0]
    row = lambda i: (i, 0)
    return pl.pallas_call(
        _merge_mlp_kernel,
        grid=(t // tm,),
        in_specs=[pl.BlockSpec((tm, GDN_V), row),
                  pl.BlockSpec((tm, GLA_V), row),
                  pl.BlockSpec((tm, D_MODEL), lambda i: (i, COL_GA // D_MODEL)),
                  pl.BlockSpec((tm, D_MODEL), lambda i: (i, COL_GB // D_MODEL)),
                  pl.BlockSpec((tm, D_MODEL), row),
                  _layer_spec((GDN_V, D_MODEL), layer),
                  _layer_spec((GLA_V, D_MODEL), layer),
                  _layer_spec((D_MODEL, D_MODEL), layer),
                  _const_spec((1, D_MODEL)),
                  _const_spec((1, D_MODEL)),
                  _layer_spec((D_MODEL, D_FF), layer),
                  _layer_spec((D_FF, D_MODEL), layer),
                  _const_spec((1, D_MODEL))],
        out_specs=pl.BlockSpec((tm, D_MODEL), row),
        out_shape=jax.ShapeDtypeStruct((t, D_MODEL), F32),
        compiler_params=pltpu.CompilerParams(
            dimension_semantics=("arbitrary",), vmem_limit_bytes=VMEM_LIMIT),
        name="merge_mlp",
    )(ya, yb, proj, proj, x2, woa, wob, wo, n_post, n_pre, wup, wdn, n_mlp)


def _reorder_w_in(w):
    o = 0
    parts = {}
    for name, width in (("aq", GDN_QK), ("ak", GDN_QK), ("av", GDN_V), ("az", GDN_V),
                        ("ab", GDN_HEADS), ("aa", GDN_HEADS), ("bq", GLA_QK), ("bk", GLA_QK),
                        ("bv", GLA_V), ("br", GLA_V), ("bg", GLA_GATE_RANK),
                        ("ga", D_MODEL), ("gb", D_MODEL)):
        parts[name] = w[..., o:o + width]
        o += width
    n_small = 2 * GDN_HEADS + GLA_GATE_RANK
    pad = jnp.zeros(w.shape[:-1] + (LANES - n_small,), w.dtype)
    out = jnp.concatenate([parts["aq"], parts["ak"], parts["av"], parts["az"], parts["bq"],
                           parts["bk"], parts["bv"], parts["br"], parts["ga"], parts["gb"],
                           parts["ab"], parts["aa"], parts["bg"], pad], axis=-1)
    return out.astype(BF16)


def _pad_row(v, offset):
    return jnp.zeros((1, LANES), F32).at[0, offset:offset + v.shape[0]].set(v)


def kernel(x, w_in, conv_w, a_log, dt_bias, gdn_norm, gla_gate_w2, gla_gate_b, gla_norm,
           w_out_a, w_out_b, w_o, norm_mix_pre, norm_mix_post, norm_mlp_pre, norm_mlp_post,
           w_mlp_up, w_mlp_down):
    batch, seq, d = x.shape
    x2 = x.reshape(batch * seq, d)
    w_in_b = _reorder_w_in(w_in)
    woa_b, wob_b, wo_b = w_out_a.astype(BF16), w_out_b.astype(BF16), w_o.astype(BF16)
    wup_b, wdn_b = w_mlp_up.astype(BF16), w_mlp_down.astype(BF16)
    w2pad = jnp.zeros((DEPTH, LANES, GLA_QK), F32).at[:, SM_GLR:SM_GLR + GLA_GATE_RANK].set(
        gla_gate_w2)
    for l in range(DEPTH):
        proj = _inproj(x2, norm_mix_pre[l][None, :], w_in_b, l)
        ya = _gdn(proj, conv_w[l], _pad_row(a_log[l], SM_DECAY), _pad_row(dt_bias[l], SM_DECAY),
                  gdn_norm[l][None, :], batch, seq)
        yb = _gla(proj, w2pad[l], gla_gate_b[l][None, :], gla_norm[l][None, :], batch, seq)
        x2 = _merge_mlp(ya, yb, proj, x2, woa_b, wob_b, wo_b,
                        norm_mix_post[l][None, :], norm_mlp_pre[l][None, :],
                        wup_b, wdn_b, norm_mlp_post[l][None, :], l)
    return x2.reshape(batch, seq, d)
```

```python
import functools

import jax
import jax.numpy as jnp
from jax import lax
from jax.experimental import pallas as pl
from jax.experimental.pallas import tpu as pltpu

D_MODEL = 1024
DEPTH = 4
GDN_HEADS = 4
GDN_DK = 128
GDN_DV = 128
CONV_WIDTH = 4
GLA_HEADS = 4
GLA_DK = 128
GLA_DV = 256
GLA_GATE_RANK = 16
GLA_TAU = 16.0
CHUNK = 64
D_FF = 4 * D_MODEL
EPS = 1e-6

GDN_QK = GDN_HEADS * GDN_DK
GDN_V = GDN_HEADS * GDN_DV
GLA_QK = GLA_HEADS * GLA_DK
GLA_V = GLA_HEADS * GLA_DV

LANES = 128
SUBLANES = 8
CARRY_ROWS = SUBLANES

GDN_COLS = 2 * GDN_QK + 2 * GDN_V
GLA_COLS = 2 * GLA_QK + 2 * GLA_V
GATE_COLS = 2 * D_MODEL
OFF_GDN = 0
OFF_BETA = OFF_GDN + GDN_COLS
OFF_GLA = OFF_BETA + 2 * GDN_HEADS
OFF_GLR = OFF_GLA + GLA_COLS
OFF_GATE = OFF_GLR + GLA_GATE_RANK
GDN_CONV_COLS = 2 * GDN_QK + GDN_V
SM_BETA = 0
SM_DECAY = GDN_HEADS
SM_GLR = 2 * GDN_HEADS

VMEM_LIMIT = 56 * 1024 * 1024

F32 = jnp.float32
BF16 = jnp.bfloat16


def _dot(a, b):
    return jnp.dot(a, b, preferred_element_type=F32)


def _dot_nt(a, b):
    return lax.dot_general(a, b, (((1,), (1,)), ((), ())), preferred_element_type=F32)


def _bf(x):
    return x.astype(BF16)


def _split3(x):
    hi = _bf(x)
    r = x - hi.astype(F32)
    mid = _bf(r)
    lo = _bf(r - mid.astype(F32))
    return hi, mid, lo


def _dot_exact_lhs(a_bf, x):
    n = x.shape[1]
    y = _dot(a_bf, jnp.concatenate(_split3(x), axis=1))
    return (y[:, :n] + y[:, n:2 * n]) + y[:, 2 * n:]


def _dot_x3(a, b):
    a_hi = _bf(a)
    a_lo = _bf(a - a_hi.astype(F32))
    b_hi = _bf(b)
    b_lo = _bf(b - b_hi.astype(F32))
    return (_dot(a_hi, b_lo) + _dot(a_lo, b_hi)) + _dot(a_hi, b_hi)


def _rms(x, w):
    return x * lax.rsqrt(jnp.mean(x * x, axis=-1, keepdims=True) + EPS) * w


def _sigmoid(x):
    return 1.0 / (1.0 + jnp.exp(-x))


def _silu(x):
    return x * _sigmoid(x)


def _softplus(x):
    return jnp.maximum(x, 0.0) + jnp.log(1.0 + jnp.exp(-jnp.abs(x)))


def _const_spec(shape):
    nd = len(shape)
    return pl.BlockSpec(shape, lambda *_: (0,) * nd, pipeline_mode=pl.Buffered(1))


def _layer_spec(shape, layer):
    nd = len(shape)
    return pl.BlockSpec((None,) + tuple(shape), lambda *_: (layer,) + (0,) * nd,
                        pipeline_mode=pl.Buffered(1))


def _tri_masks():
    row = lax.broadcasted_iota(jnp.int32, (CHUNK, CHUNK), 0)
    col = lax.broadcasted_iota(jnp.int32, (CHUNK, CHUNK), 1)
    return row >= col, row > col


def _inproj_kernel(x_ref, nw_ref, wa_ref, wb_ref, wg_ref, ws_ref, oa_ref, ob_ref, og_ref, os_ref):
    h = _bf(_rms(x_ref[...], nw_ref[...]))
    oa_ref[...] = _dot(h, wa_ref[...])
    ob_ref[...] = _dot(h, wb_ref[...])
    og_ref[...] = _dot(h, wg_ref[...])
    os_ref[...] = _dot(h, ws_ref[...])


def _inproj(x2, nw, wa, wb, wg, ws, layer, tm=256):
    t = x2.shape[0]
    row = lambda i: (i, 0)
    widths = (GDN_COLS, GLA_COLS, GATE_COLS, LANES)
    return pl.pallas_call(
        _inproj_kernel,
        grid=(t // tm,),
        in_specs=[pl.BlockSpec((tm, D_MODEL), row), _const_spec((1, D_MODEL))]
        + [_layer_spec((D_MODEL, w), layer) for w in widths],
        out_specs=[pl.BlockSpec((tm, w), row) for w in widths],
        out_shape=[jax.ShapeDtypeStruct((t, w), F32) for w in widths],
        compiler_params=pltpu.CompilerParams(
            dimension_semantics=("arbitrary",), vmem_limit_bytes=VMEM_LIMIT),
        name="inproj",
    )(x2, nw, wa, wb, wg, ws)


def _gdn_kernel(qkv_ref, z_ref, sm_ref, cw_ref, alog_ref, dt_ref, nw_ref, o_ref,
                state_ref, carry_ref, act_ref, beta_ref, g_ref,
                tinv_ref, kg_ref, qd_ref, kdt_ref, qk_ref, egl_ref, *, nb, ts, solve_group):
    n_chunks = ts // CHUNK

    @pl.when(pl.program_id(1) == 0)
    def _():
        state_ref[...] = jnp.zeros_like(state_ref)
        carry_ref[...] = jnp.zeros_like(carry_ref)

    for bb in range(nb):
        for j in range(GDN_CONV_COLS // LANES):
            cs = slice(j * LANES, (j + 1) * LANES)
            x = qkv_ref[bb, :, cs]
            xx = jnp.concatenate([carry_ref[bb, :, cs], x], axis=0)
            base = CARRY_ROWS - (CONV_WIDTH - 1)
            acc = xx[base:base + ts] * cw_ref[0:1, cs]
            for k in range(1, CONV_WIDTH):
                acc = acc + xx[base + k:base + k + ts] * cw_ref[k:k + 1, cs]
            carry_ref[bb, :, cs] = x[ts - CARRY_ROWS:ts]
            a = _silu(acc)
            if j < 2 * GDN_HEADS:
                a = a * lax.rsqrt(jnp.sum(a * a, axis=-1, keepdims=True) + EPS)
                if j < GDN_HEADS:
                    a = a * (GDN_DK ** -0.5)
            act_ref[bb, :, cs] = a
        sm = sm_ref[bb]
        beta_ref[bb] = _sigmoid(sm)
        g_ref[bb] = -(jnp.exp(alog_ref[...]) * _softplus(sm + dt_ref[...]))

    causal, strict = _tri_masks()
    tril = _bf(jnp.where(causal, 1.0, 0.0))
    eye = jnp.where(causal & jnp.logical_not(strict), 1.0, 0.0).astype(F32)
    rrow = lax.broadcasted_iota(jnp.int32, (CHUNK, 4 * CHUNK), 0)
    rcol = lax.broadcasted_iota(jnp.int32, (CHUNK, 4 * CHUNK), 1)
    rhs_mask = ((rcol < CHUNK) & (rrow > rcol)) | (rcol >= 2 * CHUNK)
    hv = [slice(h * GDN_DV, (h + 1) * GDN_DV) for h in range(GDN_HEADS)]

    def solve_body(grp, carry):
        insts = [(bb, ci, h) for bb in range(nb) for ci in range(solve_group)
                 for h in range(GDN_HEADS)]
        n = len(insts)
        chunk_ids = [grp * solve_group + ci for ci in range(solve_group)]
        rows = [pl.ds(pl.multiple_of(c * CHUNK, CHUNK), CHUNK) for c in chunk_ids]
        beta_c = {(bb, ci): beta_ref[bb, rows[ci], :] for bb in range(nb)
                  for ci in range(solve_group)}
        g_c = {(bb, ci): g_ref[bb, rows[ci], :] for bb in range(nb) for ci in range(solve_group)}

        dg = [_dot_exact_lhs(tril, jnp.where(
            rhs_mask, g_c[bb, ci][:, SM_DECAY + h:SM_DECAY + h + 1], 0.0)) for bb, ci, h in insts]
        q = [act_ref[bb, rows[ci], h * GDN_DK:(h + 1) * GDN_DK] for bb, ci, h in insts]
        k = [act_ref[bb, rows[ci], GDN_QK + h * GDN_DK:GDN_QK + (h + 1) * GDN_DK]
             for bb, ci, h in insts]
        kb = [k[i] * beta_c[bb, ci][:, SM_BETA + h:SM_BETA + h + 1]
              for i, (bb, ci, h) in enumerate(insts)]
        kq = [_dot_nt(_bf(jnp.concatenate([kb[i], q[i]], axis=0)), _bf(k[i])) for i in range(n)]
        decay = [jnp.where(causal, jnp.exp(d[:, :CHUNK]), 0.0) for d in dg]
        gc = [d[:, 2 * CHUNK:] for d in dg]
        egc = [jnp.exp(x) for x in gc]
        a_mat = [jnp.where(strict, kq[i][:CHUNK] * decay[i], 0.0) for i in range(n)]
        p = [_bf(x) for x in a_mat]
        tinv = [eye - x for x in a_mat]
        power = 2
        while True:
            pp = [_dot(x, x) for x in p]
            p = [_bf(x) for x in pp]
            xp = [_dot(_bf(tinv[i]), p[i]) for i in range(n)]
            tinv = [tinv[i] + xp[i] for i in range(n)]
            power *= 2
            if power >= CHUNK:
                break
        for i, (bb, ci, h) in enumerate(insts):
            idx = (bb * n_chunks + chunk_ids[ci]) * GDN_HEADS + h
            gl = gc[i][CHUNK - 1:CHUNK, :]
            kg = k[i] * egc[i]
            kg_hi = _bf(kg)
            tinv_ref[idx] = _bf(tinv[i])
            kg_ref[idx, :CHUNK, :] = kg_hi
            kg_ref[idx, CHUNK:, :] = _bf(kg - kg_hi.astype(F32))
            qd_ref[bb, rows[ci], hv[h]] = _bf(q[i] * egc[i])
            kdt_ref[idx] = _bf((k[i] * jnp.exp(gl - gc[i])).T)
            qk_ref[bb * GDN_HEADS + h, rows[ci], :] = _bf(kq[i][CHUNK:] * decay[i])
            egl_ref[pl.ds(pl.multiple_of(idx * SUBLANES, SUBLANES), SUBLANES), :] = (
                jnp.broadcast_to(jnp.exp(gl), (SUBLANES, LANES)))
        return carry

    lax.fori_loop(0, n_chunks // solve_group, solve_body, 0)

    nw = nw_ref[...]
    chains = [(bb, h) for bb in range(nb) for h in range(GDN_HEADS)]
    nc = range(len(chains))
    states = [state_ref[bb * GDN_HEADS + h] for bb, h in chains]
    for c in range(n_chunks):
        rows = slice(c * CHUNK, (c + 1) * CHUNK)
        idx = [(bb * n_chunks + c) * GDN_HEADS + h for bb, h in chains]
        s_hi = [_bf(states[j]) for j in nc]
        s_lo = [_bf(states[j] - s_hi[j].astype(F32)) for j in nc]
        ks_a = [_dot(kg_ref[idx[j]], s_hi[j]) for j in nc]
        ks_b = [_dot(kg_ref[idx[j], :CHUNK, :], s_lo[j]) for j in nc]
        resid = [(act_ref[bb, rows, 2 * GDN_QK + h * GDN_DV:2 * GDN_QK + (h + 1) * GDN_DV]
                  - ((ks_b[j] + ks_a[j][CHUNK:]) + ks_a[j][:CHUNK]))
                 * beta_ref[bb, rows, SM_BETA + h:SM_BETA + h + 1]
                 for j, (bb, h) in enumerate(chains)]
        v_new = [_bf(_dot(tinv_ref[idx[j]], _bf(resid[j]))) for j in nc]
        ds = [_dot(kdt_ref[idx[j]], v_new[j]) for j in nc]
        states = [states[j] * egl_ref[idx[j] * SUBLANES:idx[j] * SUBLANES + 1, :] + ds[j]
                  for j in nc]
        o = [_dot(qd_ref[bb, rows, hv[h]], s_hi[j])
             + _dot(qk_ref[bb * GDN_HEADS + h, rows, :], v_new[j])
             for j, (bb, h) in enumerate(chains)]
        for j, (bb, h) in enumerate(chains):
            o_ref[bb, rows, hv[h]] = _rms(o[j], nw) * _silu(z_ref[bb, rows, hv[h]])
    for j, (bb, h) in enumerate(chains):
        state_ref[bb * GDN_HEADS + h] = states[j]


def _gdn(pa, sm, cw, alog_row, dt_row, nw, batch, seq, nb=2, ts=512, solve_group=2):
    ns = seq // ts
    n_chunks = ts // CHUNK
    n_inst = nb * n_chunks * GDN_HEADS
    pa3 = pa.reshape(batch, seq, GDN_COLS)
    sm3 = sm.reshape(batch, seq, LANES)
    out = pl.pallas_call(
        functools.partial(_gdn_kernel, nb=nb, ts=ts, solve_group=solve_group),
        grid=(batch // nb, ns),
        in_specs=[pl.BlockSpec((nb, ts, GDN_CONV_COLS), lambda b, s: (b, s, 0)),
                  pl.BlockSpec((nb, ts, GDN_V), lambda b, s: (b, s, GDN_CONV_COLS // GDN_V)),
                  pl.BlockSpec((nb, ts, LANES), lambda b, s: (b, s, 0)),
                  _const_spec((CONV_WIDTH, GDN_CONV_COLS)),
                  _const_spec((1, LANES)),
                  _const_spec((1, LANES)),
                  _const_spec((1, GDN_DV))],
        out_specs=pl.BlockSpec((nb, ts, GDN_V), lambda b, s: (b, s, 0)),
        out_shape=jax.ShapeDtypeStruct((batch, seq, GDN_V), F32),
        scratch_shapes=[pltpu.VMEM((nb * GDN_HEADS, GDN_DK, GDN_DV), F32),
                        pltpu.VMEM((nb, CARRY_ROWS, GDN_CONV_COLS), F32),
                        pltpu.VMEM((nb, ts, GDN_CONV_COLS), F32),
                        pltpu.VMEM((nb, ts, LANES), F32),
                        pltpu.VMEM((nb, ts, LANES), F32),
                        pltpu.VMEM((n_inst, CHUNK, CHUNK), BF16),
                        pltpu.VMEM((n_inst, 2 * CHUNK, GDN_DK), BF16),
                        pltpu.VMEM((nb, ts, GDN_QK), BF16),
                        pltpu.VMEM((n_inst, GDN_DK, CHUNK), BF16),
                        pltpu.VMEM((nb * GDN_HEADS, ts, CHUNK), BF16),
                        pltpu.VMEM((n_inst * SUBLANES, LANES), F32)],
        compiler_params=pltpu.CompilerParams(
            dimension_semantics=("arbitrary", "arbitrary"), vmem_limit_bytes=VMEM_LIMIT),
        name="gdn_mixer",
    )(pa3, pa3, sm3, cw, alog_row, dt_row, nw)
    return out.reshape(batch * seq, GDN_V)


def _gla_kernel(q_ref, k_ref, v_ref, r_ref, sm_ref, w2_ref, gb_ref, nw_ref, o_ref,
                state_ref, la_ref, *, ts, group):
    @pl.when(pl.program_id(1) == 0)
    def _():
        state_ref[...] = jnp.zeros_like(state_ref)

    logit = _dot_x3(sm_ref[...], w2_ref[...]) + gb_ref[...]
    la_ref[...] = -_softplus(-logit) * (1.0 / GLA_TAU)

    causal, _ = _tri_masks()
    tril = _bf(jnp.where(causal, 1.0, 0.0))
    nw = nw_ref[...]

    def group_body(grp, carry):
        insts = [(ci, h) for ci in range(group) for h in range(GLA_HEADS)]
        n = len(insts)
        rows = [pl.ds(pl.multiple_of((grp * group + ci) * CHUNK, CHUNK), CHUNK)
                for ci in range(group)]
        kc = [slice(h * GLA_DK, (h + 1) * GLA_DK) for h in range(GLA_HEADS)]
        vc = [slice(h * GLA_DV, (h + 1) * GLA_DV) for h in range(GLA_HEADS)]
        b_all = [_dot_exact_lhs(tril, la_ref[r, :]) for r in rows]
        b = [b_all[ci][:, kc[h]] for ci, h in insts]
        q = [q_ref[rows[ci], kc[h]] * (GLA_DK ** -0.5) for ci, h in insts]
        k = [k_ref[rows[ci], kc[h]] for ci, h in insts]
        v = [v_ref[rows[ci], vc[h]] for ci, h in insts]
        b_last = [x[CHUNK - 1:CHUNK, :] for x in b]
        b_mid = [x[CHUNK // 2:CHUNK // 2 + 1, :] for x in b]
        attn = [_dot_nt(_bf(q[i] * jnp.exp(b[i] - b_mid[i])), _bf(k[i] * jnp.exp(b_mid[i] - b[i])))
                for i in range(n)]
        d_state = [_dot(_bf(v[i].T), _bf(k[i] * jnp.exp(b_last[i] - b[i]))) for i in range(n)]
        o_intra = [_dot(_bf(jnp.where(causal, attn[i], 0.0)), _bf(v[i])) for i in range(n)]
        states = [state_ref[h] for h in range(GLA_HEADS)]
        o_inter = [None] * n
        for i, (ci, h) in enumerate(insts):
            o_inter[i] = _dot_nt(_bf(q[i] * jnp.exp(b[i])), _bf(states[h]))
            states[h] = states[h] * jnp.exp(b_last[i]) + d_state[i]
        for h in range(GLA_HEADS):
            state_ref[h] = states[h]
        for i, (ci, h) in enumerate(insts):
            o_ref[rows[ci], vc[h]] = _rms(o_intra[i] + o_inter[i], nw) * _silu(r_ref[rows[ci], vc[h]])
        return carry

    lax.fori_loop(0, ts // (CHUNK * group), group_body, 0)


def _gla(pb, sm, w2pad, gate_b, nw, batch, seq, ts=512, group=2):
    ns = seq // ts
    rowblk = lambda j: (lambda b, s: (b * ns + s, j))
    return pl.pallas_call(
        functools.partial(_gla_kernel, ts=ts, group=group),
        grid=(batch, ns),
        in_specs=[pl.BlockSpec((ts, GLA_QK), rowblk(0)),
                  pl.BlockSpec((ts, GLA_QK), rowblk(1)),
                  pl.BlockSpec((ts, GLA_V), rowblk(2 * GLA_QK // GLA_V)),
                  pl.BlockSpec((ts, GLA_V), rowblk(2 * GLA_QK // GLA_V + 1)),
                  pl.BlockSpec((ts, LANES), rowblk(0)),
                  _const_spec((LANES, GLA_QK)),
                  _const_spec((1, GLA_QK)),
                  _const_spec((1, GLA_DV))],
        out_specs=pl.BlockSpec((ts, GLA_V), rowblk(0)),
        out_shape=jax.ShapeDtypeStruct((batch * seq, GLA_V), F32),
        scratch_shapes=[pltpu.VMEM((GLA_HEADS, GLA_DV, GLA_DK), F32),
                        pltpu.VMEM((ts, GLA_QK), F32)],
        compiler_params=pltpu.CompilerParams(
            dimension_semantics=("arbitrary", "arbitrary"), vmem_limit_bytes=VMEM_LIMIT),
        name="gla_mixer",
    )(pb, pb, pb, pb, sm, w2pad, gate_b, nw)


def _merge_mlp_kernel(ya_ref, yb_ref, ga_ref, gb_ref, x_ref, woa_ref, wob_ref, wo_ref,
                      n_post_ref, n_pre_ref, wup_ref, wdn_ref, n_mlp_ref, o_ref):
    ya = _dot(_bf(ya_ref[...]), woa_ref[...])
    yb = _dot(_bf(yb_ref[...]), wob_ref[...])
    merged = _sigmoid(ga_ref[...]) * ya + _sigmoid(gb_ref[...]) * yb
    y = _dot(_bf(merged), wo_ref[...])
    x1 = x_ref[...] + _rms(y, n_post_ref[...])
    h = _bf(_rms(x1, n_pre_ref[...]))
    acc = jnp.zeros(x1.shape, F32)
    for c in range(D_FF // D_MODEL):
        cs = slice(c * D_MODEL, (c + 1) * D_MODEL)
        u = jnp.square(jnp.maximum(_dot(h, wup_ref[:, cs]), 0.0))
        acc = acc + _dot(_bf(u), wdn_ref[cs, :])
    o_ref[...] = x1 + _rms(acc, n_mlp_ref[...])


def _merge_mlp(ya, yb, pg, x2, woa, wob, wo, n_post, n_pre, wup, wdn, n_mlp, layer, tm=256):
    t = x2.shape[0]
    row = lambda i: (i, 0)
    return pl.pallas_call(
        _merge_mlp_kernel,
        grid=(t // tm,),
        in_specs=[pl.BlockSpec((tm, GDN_V), row),
                  pl.BlockSpec((tm, GLA_V), row),
                  pl.BlockSpec((tm, D_MODEL), lambda i: (i, 0)),
                  pl.BlockSpec((tm, D_MODEL), lambda i: (i, 1)),
                  pl.BlockSpec((tm, D_MODEL), row),
                  _layer_spec((GDN_V, D_MODEL), layer),
                  _layer_spec((GLA_V, D_MODEL), layer),
                  _layer_spec((D_MODEL, D_MODEL), layer),
                  _const_spec((1, D_MODEL)),
                  _const_spec((1, D_MODEL)),
                  _layer_spec((D_MODEL, D_FF), layer),
                  _layer_spec((D_FF, D_MODEL), layer),
                  _const_spec((1, D_MODEL))],
        out_specs=pl.BlockSpec((tm, D_MODEL), row),
        out_shape=jax.ShapeDtypeStruct((t, D_MODEL), F32),
        compiler_params=pltpu.CompilerParams(
            dimension_semantics=("arbitrary",), vmem_limit_bytes=VMEM_LIMIT),
        name="merge_mlp",
    )(ya, yb, pg, pg, x2, woa, wob, wo, n_post, n_pre, wup, wdn, n_mlp)


def _pad_row(v, offset):
    return jnp.zeros((1, LANES), F32).at[0, offset:offset + v.shape[0]].set(v)


def kernel(x, w_in, conv_w, a_log, dt_bias, gdn_norm, gla_gate_w2, gla_gate_b, gla_norm,
           w_out_a, w_out_b, w_o, norm_mix_pre, norm_mix_post, norm_mlp_pre, norm_mlp_post,
           w_mlp_up, w_mlp_down):
    batch, seq, d = x.shape
    x2 = x.reshape(batch * seq, d)
    w_a = w_in[:, :, OFF_GDN:OFF_GDN + GDN_COLS].astype(BF16)
    w_b = w_in[:, :, OFF_GLA:OFF_GLA + GLA_COLS].astype(BF16)
    w_g = w_in[:, :, OFF_GATE:OFF_GATE + GATE_COLS].astype(BF16)
    n_small = 2 * GDN_HEADS + GLA_GATE_RANK
    w_s = jnp.concatenate(
        [w_in[:, :, OFF_BETA:OFF_BETA + 2 * GDN_HEADS], w_in[:, :, OFF_GLR:OFF_GLR + GLA_GATE_RANK],
         jnp.zeros((DEPTH, d, LANES - n_small), w_in.dtype)], axis=-1).astype(BF16)
    woa_b, wob_b, wo_b = w_out_a.astype(BF16), w_out_b.astype(BF16), w_o.astype(BF16)
    wup_b, wdn_b = w_mlp_up.astype(BF16), w_mlp_down.astype(BF16)
    w2pad = jnp.zeros((DEPTH, LANES, GLA_QK), F32).at[:, SM_GLR:SM_GLR + GLA_GATE_RANK].set(
        gla_gate_w2)
    for l in range(DEPTH):
        pa, pb, pg, sm = _inproj(x2, norm_mix_pre[l][None, :], w_a, w_b, w_g, w_s, l)
        ya = _gdn(pa, sm, conv_w[l], _pad_row(a_log[l], SM_DECAY), _pad_row(dt_bias[l], SM_DECAY),
                  gdn_norm[l][None, :], batch, seq)
        yb = _gla(pb, sm, w2pad[l], gla_gate_b[l][None, :], gla_norm[l][None, :], batch, seq)
        x2 = _merge_mlp(ya, yb, pg, x2, woa_b, wob_b, wo_b,
                        norm_mix_post[l][None, :], norm_mlp_pre[l][None, :],
                        wup_b, wdn_b, norm_mlp_post[l][None, :], l)
    return x2.reshape(batch, seq, d)
```

```python
import functools

import jax
import jax.numpy as jnp
from jax import lax
from jax.experimental import pallas as pl
from jax.experimental.pallas import tpu as pltpu

D_MODEL = 1024
DEPTH = 4
GDN_HEADS = 4
GDN_DK = 128
GDN_DV = 128
CONV_WIDTH = 4
GLA_HEADS = 4
GLA_DK = 128
GLA_DV = 256
GLA_GATE_RANK = 16
GLA_TAU = 16.0
CHUNK = 64
D_FF = 4 * D_MODEL
EPS = 1e-6

GDN_QK = GDN_HEADS * GDN_DK
GDN_V = GDN_HEADS * GDN_DV
GLA_QK = GLA_HEADS * GLA_DK
GLA_V = GLA_HEADS * GLA_DV

LANES = 128
SUBLANES = 8
CARRY_ROWS = SUBLANES

GDN_COLS = 2 * GDN_QK + 2 * GDN_V
GLA_COLS = 2 * GLA_QK + 2 * GLA_V
GATE_COLS = 2 * D_MODEL
OFF_GDN = 0
OFF_BETA = OFF_GDN + GDN_COLS
OFF_GLA = OFF_BETA + 2 * GDN_HEADS
OFF_GLR = OFF_GLA + GLA_COLS
OFF_GATE = OFF_GLR + GLA_GATE_RANK
GDN_CONV_COLS = 2 * GDN_QK + GDN_V
SM_BETA = 0
SM_DECAY = GDN_HEADS
SM_GLR = 2 * GDN_HEADS

VMEM_LIMIT = 56 * 1024 * 1024

F32 = jnp.float32
BF16 = jnp.bfloat16


def _dot(a, b):
    return jnp.dot(a, b, preferred_element_type=F32)


def _dot_nt(a, b):
    return lax.dot_general(a, b, (((1,), (1,)), ((), ())), preferred_element_type=F32)


def _bf(x):
    return x.astype(BF16)


def _split3(x):
    hi = _bf(x)
    r = x - hi.astype(F32)
    mid = _bf(r)
    lo = _bf(r - mid.astype(F32))
    return hi, mid, lo


def _dot_exact_lhs(a_bf, x):
    n = x.shape[1]
    y = _dot(a_bf, jnp.concatenate(_split3(x), axis=1))
    return (y[:, :n] + y[:, n:2 * n]) + y[:, 2 * n:]


def _dot_x3(a, b):
    a_hi = _bf(a)
    a_lo = _bf(a - a_hi.astype(F32))
    b_hi = _bf(b)
    b_lo = _bf(b - b_hi.astype(F32))
    return (_dot(a_hi, b_lo) + _dot(a_lo, b_hi)) + _dot(a_hi, b_hi)


def _rms(x, w):
    return x * lax.rsqrt(jnp.mean(x * x, axis=-1, keepdims=True) + EPS) * w


def _sigmoid(x):
    return 1.0 / (1.0 + jnp.exp(-x))


def _silu(x):
    return x * _sigmoid(x)


def _softplus(x):
    return jnp.maximum(x, 0.0) + jnp.log(1.0 + jnp.exp(-jnp.abs(x)))


def _const_spec(shape):
    nd = len(shape)
    return pl.BlockSpec(shape, lambda *_: (0,) * nd, pipeline_mode=pl.Buffered(1))


def _layer_spec(shape, layer):
    nd = len(shape)
    return pl.BlockSpec((None,) + tuple(shape), lambda *_: (layer,) + (0,) * nd,
                        pipeline_mode=pl.Buffered(1))


def _tri_masks():
    row = lax.broadcasted_iota(jnp.int32, (CHUNK, CHUNK), 0)
    col = lax.broadcasted_iota(jnp.int32, (CHUNK, CHUNK), 1)
    return row >= col, row > col


def _gdn_kernel(x_ref, npre_ref, wa_ref, ws_ref, cw_ref, alog_ref, dt_ref, nw_ref, o_ref,
                state_ref, carry_ref, h_ref, act_ref, z_ref, beta_ref, g_ref,
                tinv_ref, kg_ref, qd_ref, kdt_ref, qk_ref, egl_ref, *, nb, ts, solve_group):
    n_chunks = ts // CHUNK

    @pl.when(pl.program_id(1) == 0)
    def _():
        state_ref[...] = jnp.zeros_like(state_ref)
        carry_ref[...] = jnp.zeros_like(carry_ref)

    tile = 2 * LANES
    for bb in range(nb):
        h_ref[bb] = _bf(_rms(x_ref[bb], npre_ref[...]))
    for bb in range(nb):
        sm = _dot(h_ref[bb], ws_ref[...])
        beta_ref[bb] = _sigmoid(sm)
        g_ref[bb] = -(jnp.exp(alog_ref[...]) * _softplus(sm + dt_ref[...]))
        for t in range(GDN_COLS // tile):
            proj = _dot(h_ref[bb], wa_ref[:, t * tile:(t + 1) * tile])
            if t * tile >= GDN_CONV_COLS:
                z_ref[bb, :, t * tile - GDN_CONV_COLS:(t + 1) * tile - GDN_CONV_COLS] = proj
                continue
            for j in range(t * tile // LANES, (t + 1) * tile // LANES):
                cs = slice(j * LANES, (j + 1) * LANES)
                x = proj[:, j * LANES - t * tile:(j + 1) * LANES - t * tile]
                xx = jnp.concatenate([carry_ref[bb, :, cs], x], axis=0)
                base = CARRY_ROWS - (CONV_WIDTH - 1)
                acc = xx[base:base + ts] * cw_ref[0:1, cs]
                for k in range(1, CONV_WIDTH):
                    acc = acc + xx[base + k:base + k + ts] * cw_ref[k:k + 1, cs]
                carry_ref[bb, :, cs] = x[ts - CARRY_ROWS:ts]
                a = _silu(acc)
                if j < 2 * GDN_HEADS:
                    a = a * lax.rsqrt(jnp.sum(a * a, axis=-1, keepdims=True) + EPS)
                    if j < GDN_HEADS:
                        a = a * (GDN_DK ** -0.5)
                act_ref[bb, :, cs] = a

    causal, strict = _tri_masks()
    tril = _bf(jnp.where(causal, 1.0, 0.0))
    eye = jnp.where(causal & jnp.logical_not(strict), 1.0, 0.0).astype(F32)
    rrow = lax.broadcasted_iota(jnp.int32, (CHUNK, 4 * CHUNK), 0)
    rcol = lax.broadcasted_iota(jnp.int32, (CHUNK, 4 * CHUNK), 1)
    rhs_mask = ((rcol < CHUNK) & (rrow > rcol)) | (rcol >= 2 * CHUNK)
    hv = [slice(h * GDN_DV, (h + 1) * GDN_DV) for h in range(GDN_HEADS)]

    def solve_body(grp, carry):
        insts = [(bb, ci, h) for bb in range(nb) for ci in range(solve_group)
                 for h in range(GDN_HEADS)]
        n = len(insts)
        chunk_ids = [grp * solve_group + ci for ci in range(solve_group)]
        rows = [pl.ds(pl.multiple_of(c * CHUNK, CHUNK), CHUNK) for c in chunk_ids]
        beta_c = {(bb, ci): beta_ref[bb, rows[ci], :] for bb in range(nb)
                  for ci in range(solve_group)}
        g_c = {(bb, ci): g_ref[bb, rows[ci], :] for bb in range(nb) for ci in range(solve_group)}

        dg = [_dot_exact_lhs(tril, jnp.where(
            rhs_mask, g_c[bb, ci][:, SM_DECAY + h:SM_DECAY + h + 1], 0.0)) for bb, ci, h in insts]
        q = [act_ref[bb, rows[ci], h * GDN_DK:(h + 1) * GDN_DK] for bb, ci, h in insts]
        k = [act_ref[bb, rows[ci], GDN_QK + h * GDN_DK:GDN_QK + (h + 1) * GDN_DK]
             for bb, ci, h in insts]
        kb = [k[i] * beta_c[bb, ci][:, SM_BETA + h:SM_BETA + h + 1]
              for i, (bb, ci, h) in enumerate(insts)]
        kq = [_dot_nt(_bf(jnp.concatenate([kb[i], q[i]], axis=0)), _bf(k[i])) for i in range(n)]
        decay = [jnp.where(causal, jnp.exp(d[:, :CHUNK]), 0.0) for d in dg]
        gc = [d[:, 2 * CHUNK:] for d in dg]
        egc = [jnp.exp(x) for x in gc]
        a_mat = [jnp.where(strict, kq[i][:CHUNK] * decay[i], 0.0) for i in range(n)]
        p = [_bf(x) for x in a_mat]
        tinv = [eye - x for x in a_mat]
        power = 2
        while True:
            pp = [_dot(x, x) for x in p]
            p = [_bf(x) for x in pp]
            xp = [_dot(_bf(tinv[i]), p[i]) for i in range(n)]
            tinv = [tinv[i] + xp[i] for i in range(n)]
            power *= 2
            if power >= CHUNK:
                break
        for i, (bb, ci, h) in enumerate(insts):
            idx = (bb * n_chunks + chunk_ids[ci]) * GDN_HEADS + h
            gl = gc[i][CHUNK - 1:CHUNK, :]
            kg = k[i] * egc[i]
            kg_hi = _bf(kg)
            tinv_ref[idx] = _bf(tinv[i])
            kg_ref[idx, :CHUNK, :] = kg_hi
            kg_ref[idx, CHUNK:, :] = _bf(kg - kg_hi.astype(F32))
            qd_ref[bb, rows[ci], hv[h]] = _bf(q[i] * egc[i])
            kdt_ref[idx] = _bf((k[i] * jnp.exp(gl - gc[i])).T)
            qk_ref[bb * GDN_HEADS + h, rows[ci], :] = _bf(kq[i][CHUNK:] * decay[i])
            egl_ref[pl.ds(pl.multiple_of(idx * SUBLANES, SUBLANES), SUBLANES), :] = (
                jnp.broadcast_to(jnp.exp(gl), (SUBLANES, LANES)))
        return carry

    lax.fori_loop(0, n_chunks // solve_group, solve_body, 0)

    nw = nw_ref[...]
    chains = [(bb, h) for bb in range(nb) for h in range(GDN_HEADS)]
    nc = range(len(chains))
    states = [state_ref[bb * GDN_HEADS + h] for bb, h in chains]
    for c in range(n_chunks):
        rows = slice(c * CHUNK, (c + 1) * CHUNK)
        idx = [(bb * n_chunks + c) * GDN_HEADS + h for bb, h in chains]
        s_hi = [_bf(states[j]) for j in nc]
        s_lo = [_bf(states[j] - s_hi[j].astype(F32)) for j in nc]
        ks_a = [_dot(kg_ref[idx[j]], s_hi[j]) for j in nc]
        ks_b = [_dot(kg_ref[idx[j], :CHUNK, :], s_lo[j]) for j in nc]
        resid = [(act_ref[bb, rows, 2 * GDN_QK + h * GDN_DV:2 * GDN_QK + (h + 1) * GDN_DV]
                  - ((ks_b[j] + ks_a[j][CHUNK:]) + ks_a[j][:CHUNK]))
                 * beta_ref[bb, rows, SM_BETA + h:SM_BETA + h + 1]
                 for j, (bb, h) in enumerate(chains)]
        v_new = [_bf(_dot(tinv_ref[idx[j]], _bf(resid[j]))) for j in nc]
        ds = [_dot(kdt_ref[idx[j]], v_new[j]) for j in nc]
        states = [states[j] * egl_ref[idx[j] * SUBLANES:idx[j] * SUBLANES + 1, :] + ds[j]
                  for j in nc]
        o = [_dot(qd_ref[bb, rows, hv[h]], s_hi[j])
             + _dot(qk_ref[bb * GDN_HEADS + h, rows, :], v_new[j])
             for j, (bb, h) in enumerate(chains)]
        for j, (bb, h) in enumerate(chains):
            o_ref[bb, rows, hv[h]] = _rms(o[j], nw) * _silu(z_ref[bb, rows, hv[h]])
    for j, (bb, h) in enumerate(chains):
        state_ref[bb * GDN_HEADS + h] = states[j]


def _gdn(x2, n_pre, wa, ws, cw, alog_row, dt_row, nw, layer, batch, seq,
         nb=2, ts=512, solve_group=2):
    ns = seq // ts
    n_chunks = ts // CHUNK
    n_inst = nb * n_chunks * GDN_HEADS
    x3 = x2.reshape(batch, seq, D_MODEL)
    out = pl.pallas_call(
        functools.partial(_gdn_kernel, nb=nb, ts=ts, solve_group=solve_group),
        grid=(batch // nb, ns),
        in_specs=[pl.BlockSpec((nb, ts, D_MODEL), lambda b, s: (b, s, 0)),
                  _const_spec((1, D_MODEL)),
                  _layer_spec((D_MODEL, GDN_COLS), layer),
                  _layer_spec((D_MODEL, LANES), layer),
                  _const_spec((CONV_WIDTH, GDN_CONV_COLS)),
                  _const_spec((1, LANES)),
                  _const_spec((1, LANES)),
                  _const_spec((1, GDN_DV))],
        out_specs=pl.BlockSpec((nb, ts, GDN_V), lambda b, s: (b, s, 0)),
        out_shape=jax.ShapeDtypeStruct((batch, seq, GDN_V), F32),
        scratch_shapes=[pltpu.VMEM((nb * GDN_HEADS, GDN_DK, GDN_DV), F32),
                        pltpu.VMEM((nb, CARRY_ROWS, GDN_CONV_COLS), F32),
                        pltpu.VMEM((nb, ts, D_MODEL), BF16),
                        pltpu.VMEM((nb, ts, GDN_CONV_COLS), F32),
                        pltpu.VMEM((nb, ts, GDN_V), F32),
                        pltpu.VMEM((nb, ts, LANES), F32),
                        pltpu.VMEM((nb, ts, LANES), F32),
                        pltpu.VMEM((n_inst, CHUNK, CHUNK), BF16),
                        pltpu.VMEM((n_inst, 2 * CHUNK, GDN_DK), BF16),
                        pltpu.VMEM((nb, ts, GDN_QK), BF16),
                        pltpu.VMEM((n_inst, GDN_DK, CHUNK), BF16),
                        pltpu.VMEM((nb * GDN_HEADS, ts, CHUNK), BF16),
                        pltpu.VMEM((n_inst * SUBLANES, LANES), F32)],
        compiler_params=pltpu.CompilerParams(
            dimension_semantics=("arbitrary", "arbitrary"), vmem_limit_bytes=VMEM_LIMIT),
        name="gdn_mixer",
    )(x3, n_pre, wa, ws, cw, alog_row, dt_row, nw)
    return out.reshape(batch * seq, GDN_V)


PB_Q, PB_K, PB_V, PB_R = 0, GLA_QK, 2 * GLA_QK, 2 * GLA_QK + GLA_V


def _gla_kernel(x_ref, npre_ref, wb_ref, ws_ref, w2_ref, gb_ref, nw_ref, o_ref,
                state_ref, h_ref, pb_ref, b_ref, *, ts):
    n_chunks = ts // CHUNK

    @pl.when(pl.program_id(1) == 0)
    def _():
        state_ref[...] = jnp.zeros_like(state_ref)

    h_ref[...] = _bf(_rms(x_ref[...], npre_ref[...]))

    def project(col, width):
        pb_ref[:, col:col + width] = _dot(h_ref[...], wb_ref[:, col:col + width])

    sm = _dot(h_ref[...], ws_ref[...])
    logit = _dot_x3(sm, w2_ref[...]) + gb_ref[...]
    log_a = -_softplus(-logit) * (1.0 / GLA_TAU)
    causal, _ = _tri_masks()
    tril = _bf(jnp.where(causal, 1.0, 0.0))
    rows = [slice(c * CHUNK, (c + 1) * CHUNK) for c in range(n_chunks)]
    for r in rows:
        b_ref[r, :] = _dot_exact_lhs(tril, log_a[r, :])

    project(PB_Q, 2 * GLA_QK)
    project(PB_V, GLA_DV)
    project(PB_R, GLA_DV)
    nw = nw_ref[...]
    cs = range(n_chunks)
    for h in range(GLA_HEADS):
        kc = slice(h * GLA_DK, (h + 1) * GLA_DK)
        vcol = PB_V + h * GLA_DV
        rcol = PB_R + h * GLA_DV
        b = [b_ref[r, kc] for r in rows]
        q = [pb_ref[r, PB_Q + h * GLA_DK:PB_Q + (h + 1) * GLA_DK] * (GLA_DK ** -0.5) for r in rows]
        k = [pb_ref[r, PB_K + h * GLA_DK:PB_K + (h + 1) * GLA_DK] for r in rows]
        v = [pb_ref[r, vcol:vcol + GLA_DV] for r in rows]
        b_last = [x[CHUNK - 1:CHUNK, :] for x in b]
        b_mid = [x[CHUNK // 2:CHUNK // 2 + 1, :] for x in b]
        attn = [_dot_nt(_bf(q[c] * jnp.exp(b[c] - b_mid[c])), _bf(k[c] * jnp.exp(b_mid[c] - b[c])))
                for c in cs]
        if h + 1 < GLA_HEADS:
            project(vcol + GLA_DV, GLA_DV)
        d_state = [_dot(_bf(v[c].T), _bf(k[c] * jnp.exp(b_last[c] - b[c]))) for c in cs]
        o_intra = [_dot(_bf(jnp.where(causal, attn[c], 0.0)), _bf(v[c])) for c in cs]
        if h + 1 < GLA_HEADS:
            project(rcol + GLA_DV, GLA_DV)
        state = state_ref[h]
        o_inter = []
        for c in cs:
            o_inter.append(_dot_nt(_bf(q[c] * jnp.exp(b[c])), _bf(state)))
            state = state * jnp.exp(b_last[c]) + d_state[c]
        state_ref[h] = state
        for c in cs:
            o_ref[rows[c], h * GLA_DV:(h + 1) * GLA_DV] = (
                _rms(o_intra[c] + o_inter[c], nw) * _silu(pb_ref[rows[c], rcol:rcol + GLA_DV]))


def _gla(x2, n_pre, wb, ws, w2pad, gate_b, nw, layer, batch, seq, ts=512):
    ns = seq // ts
    row = lambda b, s: (b * ns + s, 0)
    return pl.pallas_call(
        functools.partial(_gla_kernel, ts=ts),
        grid=(batch, ns),
        in_specs=[pl.BlockSpec((ts, D_MODEL), row),
                  _const_spec((1, D_MODEL)),
                  _layer_spec((D_MODEL, GLA_COLS), layer),
                  _layer_spec((D_MODEL, LANES), layer),
                  _const_spec((LANES, GLA_QK)),
                  _const_spec((1, GLA_QK)),
                  _const_spec((1, GLA_DV))],
        out_specs=pl.BlockSpec((ts, GLA_V), row),
        out_shape=jax.ShapeDtypeStruct((batch * seq, GLA_V), F32),
        scratch_shapes=[pltpu.VMEM((GLA_HEADS, GLA_DV, GLA_DK), F32),
                        pltpu.VMEM((ts, D_MODEL), BF16),
                        pltpu.VMEM((ts, GLA_COLS), F32),
                        pltpu.VMEM((ts, GLA_QK), F32)],
        compiler_params=pltpu.CompilerParams(
            dimension_semantics=("arbitrary", "arbitrary"), vmem_limit_bytes=VMEM_LIMIT),
        name="gla_mixer",
    )(x2, n_pre, wb, ws, w2pad, gate_b, nw)


def _merge_mlp_kernel(ya_ref, yb_ref, x_ref, n_mix_ref, wg_ref, woa_ref, wob_ref, wo_ref,
                      n_post_ref, n_pre_ref, wup_ref, wdn_ref, n_mlp_ref, o_ref):
    x = x_ref[...]
    hg = _bf(_rms(x, n_mix_ref[...]))
    ya = _dot(_bf(ya_ref[...]), woa_ref[...])
    merged = _sigmoid(_dot(hg, wg_ref[:, :D_MODEL])) * ya
    yb = _dot(_bf(yb_ref[...]), wob_ref[...])
    merged = merged + _sigmoid(_dot(hg, wg_ref[:, D_MODEL:])) * yb
    y = _dot(_bf(merged), wo_ref[...])
    x1 = x + _rms(y, n_post_ref[...])
    h = _bf(_rms(x1, n_pre_ref[...]))
    acc = jnp.zeros(x1.shape, F32)
    for c in range(D_FF // D_MODEL):
        cs = slice(c * D_MODEL, (c + 1) * D_MODEL)
        u = jnp.square(jnp.maximum(_dot(h, wup_ref[:, cs]), 0.0))
        acc = acc + _dot(_bf(u), wdn_ref[cs, :])
    o_ref[...] = x1 + _rms(acc, n_mlp_ref[...])


def _merge_mlp(ya, yb, x2, n_mix, wg, woa, wob, wo, n_post, n_pre, wup, wdn, n_mlp, layer, tm=256):
    t = x2.shape[0]
    row = lambda i: (i, 0)
    return pl.pallas_call(
        _merge_mlp_kernel,
        grid=(t // tm,),
        in_specs=[pl.BlockSpec((tm, GDN_V), row),
                  pl.BlockSpec((tm, GLA_V), row),
                  pl.BlockSpec((tm, D_MODEL), row),
                  _const_spec((1, D_MODEL)),
                  _layer_spec((D_MODEL, GATE_COLS), layer),
                  _layer_spec((GDN_V, D_MODEL), layer),
                  _layer_spec((GLA_V, D_MODEL), layer),
                  _layer_spec((D_MODEL, D_MODEL), layer),
                  _const_spec((1, D_MODEL)),
                  _const_spec((1, D_MODEL)),
                  _layer_spec((D_MODEL, D_FF), layer),
                  _layer_spec((D_FF, D_MODEL), layer),
                  _const_spec((1, D_MODEL))],
        out_specs=pl.BlockSpec((tm, D_MODEL), row),
        out_shape=jax.ShapeDtypeStruct((t, D_MODEL), F32),
        compiler_params=pltpu.CompilerParams(
            dimension_semantics=("arbitrary",), vmem_limit_bytes=VMEM_LIMIT),
        name="merge_mlp",
    )(ya, yb, x2, n_mix, wg, woa, wob, wo, n_post, n_pre, wup, wdn, n_mlp)


def _pad_row(v, offset):
    return jnp.zeros((1, LANES), F32).at[0, offset:offset + v.shape[0]].set(v)


def kernel(x, w_in, conv_w, a_log, dt_bias, gdn_norm, gla_gate_w2, gla_gate_b, gla_norm,
           w_out_a, w_out_b, w_o, norm_mix_pre, norm_mix_post, norm_mlp_pre, norm_mlp_post,
           w_mlp_up, w_mlp_down):
    batch, seq, d = x.shape
    x2 = x.reshape(batch * seq, d)
    w_a = w_in[:, :, OFF_GDN:OFF_GDN + GDN_COLS].astype(BF16)
    w_b = w_in[:, :, OFF_GLA:OFF_GLA + GLA_COLS].astype(BF16)
    w_g = w_in[:, :, OFF_GATE:OFF_GATE + GATE_COLS].astype(BF16)
    n_small = 2 * GDN_HEADS + GLA_GATE_RANK
    w_s = jnp.concatenate(
        [w_in[:, :, OFF_BETA:OFF_BETA + 2 * GDN_HEADS], w_in[:, :, OFF_GLR:OFF_GLR + GLA_GATE_RANK],
         jnp.zeros((DEPTH, d, LANES - n_small), w_in.dtype)], axis=-1).astype(BF16)
    woa_b, wob_b, wo_b = w_out_a.astype(BF16), w_out_b.astype(BF16), w_o.astype(BF16)
    wup_b, wdn_b = w_mlp_up.astype(BF16), w_mlp_down.astype(BF16)
    w2pad = jnp.zeros((DEPTH, LANES, GLA_QK), F32).at[:, SM_GLR:SM_GLR + GLA_GATE_RANK].set(
        gla_gate_w2)
    for l in range(DEPTH):
        n_mix = norm_mix_pre[l][None, :]
        ya = _gdn(x2, n_mix, w_a, w_s, conv_w[l], _pad_row(a_log[l], SM_DECAY),
                  _pad_row(dt_bias[l], SM_DECAY), gdn_norm[l][None, :], l, batch, seq)
        yb = _gla(x2, n_mix, w_b, w_s, w2pad[l], gla_gate_b[l][None, :], gla_norm[l][None, :],
                  l, batch, seq)
        x2 = _merge_mlp(ya, yb, x2, n_mix, w_g, woa_b, wob_b, wo_b,
                        norm_mix_post[l][None, :], norm_mlp_pre[l][None, :],
                        wup_b, wdn_b, norm_mlp_post[l][None, :], l)
    return x2.reshape(batch, seq, d)
```

```python
import functools

import jax
import jax.numpy as jnp
from jax import lax
from jax.experimental import pallas as pl
from jax.experimental.pallas import tpu as pltpu

D_MODEL = 1024
DEPTH = 4
GDN_HEADS = 4
GDN_DK = 128
GDN_DV = 128
CONV_WIDTH = 4
GLA_HEADS = 4
GLA_DK = 128
GLA_DV = 256
GLA_GATE_RANK = 16
GLA_TAU = 16.0
CHUNK = 64
D_FF = 4 * D_MODEL
EPS = 1e-6

GDN_QK = GDN_HEADS * GDN_DK
GDN_V = GDN_HEADS * GDN_DV
GLA_QK = GLA_HEADS * GLA_DK
GLA_V = GLA_HEADS * GLA_DV

LANES = 128
SUBLANES = 8
CARRY_ROWS = SUBLANES

GDN_COLS = 2 * GDN_QK + 2 * GDN_V
GLA_COLS = 2 * GLA_QK + 2 * GLA_V
GATE_COLS = 2 * D_MODEL
OFF_GDN = 0
OFF_BETA = OFF_GDN + GDN_COLS
OFF_GLA = OFF_BETA + 2 * GDN_HEADS
OFF_GLR = OFF_GLA + GLA_COLS
OFF_GATE = OFF_GLR + GLA_GATE_RANK
GDN_CONV_COLS = 2 * GDN_QK + GDN_V
SM_BETA = 0
SM_DECAY = GDN_HEADS
SM_GLR = 2 * GDN_HEADS

VMEM_LIMIT = 56 * 1024 * 1024

F32 = jnp.float32
BF16 = jnp.bfloat16


def _dot(a, b):
    return jnp.dot(a, b, preferred_element_type=F32)


def _dot_nt(a, b):
    return lax.dot_general(a, b, (((1,), (1,)), ((), ())), preferred_element_type=F32)


def _bf(x):
    return x.astype(BF16)


def _split3(x):
    hi = _bf(x)
    r = x - hi.astype(F32)
    mid = _bf(r)
    lo = _bf(r - mid.astype(F32))
    return hi, mid, lo


def _dot_exact_lhs(a_bf, x):
    n = x.shape[1]
    y = _dot(a_bf, jnp.concatenate(_split3(x), axis=1))
    return (y[:, :n] + y[:, n:2 * n]) + y[:, 2 * n:]


def _dot_x3(a, b):
    a_hi = _bf(a)
    a_lo = _bf(a - a_hi.astype(F32))
    b_hi = _bf(b)
    b_lo = _bf(b - b_hi.astype(F32))
    return (_dot(a_hi, b_lo) + _dot(a_lo, b_hi)) + _dot(a_hi, b_hi)


def _rms(x, w):
    return x * lax.rsqrt(jnp.mean(x * x, axis=-1, keepdims=True) + EPS) * w


def _sigmoid(x):
    return 1.0 / (1.0 + jnp.exp(-x))


def _silu(x):
    return x * _sigmoid(x)


def _softplus(x):
    return jnp.maximum(x, 0.0) + jnp.log(1.0 + jnp.exp(-jnp.abs(x)))


def _const_spec(shape):
    nd = len(shape)
    return pl.BlockSpec(shape, lambda *_: (0,) * nd, pipeline_mode=pl.Buffered(1))


def _layer_spec(shape, layer):
    nd = len(shape)
    return pl.BlockSpec((None,) + tuple(shape), lambda *_: (layer,) + (0,) * nd,
                        pipeline_mode=pl.Buffered(1))


def _tri_masks():
    row = lax.broadcasted_iota(jnp.int32, (CHUNK, CHUNK), 0)
    col = lax.broadcasted_iota(jnp.int32, (CHUNK, CHUNK), 1)
    return row >= col, row > col


def _gdn_kernel(x_ref, npre_ref, wa_ref, ws_ref, cw_ref, alog_ref, dt_ref, nw_ref, o_ref,
                state_ref, carry_ref, h_ref, act_ref, z_ref, beta_ref, g_ref,
                tinv_ref, kg_ref, qd_ref, kdt_ref, qk_ref, egl_ref, *, nb, ts, solve_group):
    n_chunks = ts // CHUNK

    @pl.when(pl.program_id(1) == 0)
    def _():
        state_ref[...] = jnp.zeros_like(state_ref)
        carry_ref[...] = jnp.zeros_like(carry_ref)

    tile = 2 * LANES
    for bb in range(nb):
        h_ref[bb] = _bf(_rms(x_ref[bb], npre_ref[...]))
    for bb in range(nb):
        sm = _dot(h_ref[bb], ws_ref[...])
        beta_ref[bb] = _sigmoid(sm)
        g_ref[bb] = -(jnp.exp(alog_ref[...]) * _softplus(sm + dt_ref[...]))
        for t in range(GDN_COLS // tile):
            proj = _dot(h_ref[bb], wa_ref[:, t * tile:(t + 1) * tile])
            if t * tile >= GDN_CONV_COLS:
                z_ref[bb, :, t * tile - GDN_CONV_COLS:(t + 1) * tile - GDN_CONV_COLS] = proj
                continue
            for j in range(t * tile // LANES, (t + 1) * tile // LANES):
                cs = slice(j * LANES, (j + 1) * LANES)
                x = proj[:, j * LANES - t * tile:(j + 1) * LANES - t * tile]
                xx = jnp.concatenate([carry_ref[bb, :, cs], x], axis=0)
                base = CARRY_ROWS - (CONV_WIDTH - 1)
                acc = xx[base:base + ts] * cw_ref[0:1, cs]
                for k in range(1, CONV_WIDTH):
                    acc = acc + xx[base + k:base + k + ts] * cw_ref[k:k + 1, cs]
                carry_ref[bb, :, cs] = x[ts - CARRY_ROWS:ts]
                a = _silu(acc)
                if j < 2 * GDN_HEADS:
                    a = a * lax.rsqrt(jnp.sum(a * a, axis=-1, keepdims=True) + EPS)
                    if j < GDN_HEADS:
                        a = a * (GDN_DK ** -0.5)
                act_ref[bb, :, cs] = a

    causal, strict = _tri_masks()
    tril = _bf(jnp.where(causal, 1.0, 0.0))
    eye = jnp.where(causal & jnp.logical_not(strict), 1.0, 0.0).astype(F32)
    rrow = lax.broadcasted_iota(jnp.int32, (CHUNK, 4 * CHUNK), 0)
    rcol = lax.broadcasted_iota(jnp.int32, (CHUNK, 4 * CHUNK), 1)
    rhs_mask = ((rcol < CHUNK) & (rrow > rcol)) | (rcol >= 2 * CHUNK)
    hv = [slice(h * GDN_DV, (h + 1) * GDN_DV) for h in range(GDN_HEADS)]
    packed = GDN_HEADS * CHUNK
    prow = lax.broadcasted_iota(jnp.int32, (packed, packed), 0)
    pcol = lax.broadcasted_iota(jnp.int32, (packed, packed), 1)
    blk_mask = (prow // CHUNK) == (pcol // CHUNK)
    eye_p = jnp.concatenate([eye] * GDN_HEADS, axis=1)

    def solve_body(grp, carry):
        insts = [(bb, ci, h) for bb in range(nb) for ci in range(solve_group)
                 for h in range(GDN_HEADS)]
        n = len(insts)
        chunk_ids = [grp * solve_group + ci for ci in range(solve_group)]
        rows = [pl.ds(pl.multiple_of(c * CHUNK, CHUNK), CHUNK) for c in chunk_ids]
        beta_c = {(bb, ci): beta_ref[bb, rows[ci], :] for bb in range(nb)
                  for ci in range(solve_group)}
        g_c = {(bb, ci): g_ref[bb, rows[ci], :] for bb in range(nb) for ci in range(solve_group)}

        dg = [_dot_exact_lhs(tril, jnp.where(
            rhs_mask, g_c[bb, ci][:, SM_DECAY + h:SM_DECAY + h + 1], 0.0)) for bb, ci, h in insts]
        q = [act_ref[bb, rows[ci], h * GDN_DK:(h + 1) * GDN_DK] for bb, ci, h in insts]
        k = [act_ref[bb, rows[ci], GDN_QK + h * GDN_DK:GDN_QK + (h + 1) * GDN_DK]
             for bb, ci, h in insts]
        kb = [k[i] * beta_c[bb, ci][:, SM_BETA + h:SM_BETA + h + 1]
              for i, (bb, ci, h) in enumerate(insts)]
        kq = [_dot_nt(_bf(jnp.concatenate([kb[i], q[i]], axis=0)), _bf(k[i])) for i in range(n)]
        decay = [jnp.where(causal, jnp.exp(d[:, :CHUNK]), 0.0) for d in dg]
        gc = [d[:, 2 * CHUNK:] for d in dg]
        egc = [jnp.exp(x) for x in gc]
        a_mat = [jnp.where(strict, kq[i][:CHUNK] * decay[i], 0.0) for i in range(n)]
        groups = [(bb, ci) for bb in range(nb) for ci in range(solve_group)]
        ng = range(len(groups))
        a_p = [jnp.concatenate([a_mat[gi * GDN_HEADS + h] for h in range(GDN_HEADS)], axis=1)
               for gi in ng]

        def block_diag(x_bf):
            return jnp.where(blk_mask, jnp.concatenate([x_bf] * GDN_HEADS, axis=0),
                             jnp.zeros((), BF16))

        p = [_bf(x) for x in a_p]
        pd = [block_diag(x) for x in p]
        tinv = [eye_p - x for x in a_p]
        power = 2
        while True:
            pp = [_dot(p[gi], pd[gi]) for gi in ng]
            p = [_bf(x) for x in pp]
            pd = [block_diag(x) for x in p]
            xp = [_dot(_bf(tinv[gi]), pd[gi]) for gi in ng]
            tinv = [tinv[gi] + xp[gi] for gi in ng]
            power *= 2
            if power >= CHUNK:
                break
        for gi, (bb, ci) in enumerate(groups):
            tinv_ref[bb * n_chunks + chunk_ids[ci]] = _bf(tinv[gi])
        for i, (bb, ci, h) in enumerate(insts):
            idx = (bb * n_chunks + chunk_ids[ci]) * GDN_HEADS + h
            gl = gc[i][CHUNK - 1:CHUNK, :]
            kg = k[i] * egc[i]
            kg_hi = _bf(kg)
            kg_ref[idx, :CHUNK, :] = kg_hi
            kg_ref[idx, CHUNK:, :] = _bf(kg - kg_hi.astype(F32))
            qd_ref[bb, rows[ci], hv[h]] = _bf(q[i] * egc[i])
            kdt_ref[idx] = _bf((k[i] * jnp.exp(gl - gc[i])).T)
            qk_ref[bb * GDN_HEADS + h, rows[ci], :] = _bf(kq[i][CHUNK:] * decay[i])
            egl_ref[pl.ds(pl.multiple_of(idx * SUBLANES, SUBLANES), SUBLANES), :] = (
                jnp.broadcast_to(jnp.exp(gl), (SUBLANES, LANES)))
        return carry

    lax.fori_loop(0, n_chunks // solve_group, solve_body, 0)

    nw = nw_ref[...]
    chains = [(bb, h) for bb in range(nb) for h in range(GDN_HEADS)]
    nc = range(len(chains))
    states = [state_ref[bb * GDN_HEADS + h] for bb, h in chains]
    for c in range(n_chunks):
        rows = slice(c * CHUNK, (c + 1) * CHUNK)
        idx = [(bb * n_chunks + c) * GDN_HEADS + h for bb, h in chains]
        s_hi = [_bf(states[j]) for j in nc]
        s_lo = [_bf(states[j] - s_hi[j].astype(F32)) for j in nc]
        ks_a = [_dot(kg_ref[idx[j]], s_hi[j]) for j in nc]
        ks_b = [_dot(kg_ref[idx[j], :CHUNK, :], s_lo[j]) for j in nc]
        resid = [(act_ref[bb, rows, 2 * GDN_QK + h * GDN_DV:2 * GDN_QK + (h + 1) * GDN_DV]
                  - ((ks_b[j] + ks_a[j][CHUNK:]) + ks_a[j][:CHUNK]))
                 * beta_ref[bb, rows, SM_BETA + h:SM_BETA + h + 1]
                 for j, (bb, h) in enumerate(chains)]
        v_new = []
        for bb in range(nb):
            r_bf = [_bf(resid[bb * GDN_HEADS + h]) for h in range(GDN_HEADS)]
            zero = jnp.zeros((CHUNK, GDN_DV), BF16)
            rhs = jnp.concatenate(
                [jnp.concatenate([r_bf[h] if hh == h else zero for hh in range(GDN_HEADS)], axis=1)
                 for h in range(GDN_HEADS)], axis=0)
            v_all = _dot(tinv_ref[bb * n_chunks + c], rhs)
            v_new += [_bf(v_all[:, hv[h]]) for h in range(GDN_HEADS)]
        ds = [_dot(kdt_ref[idx[j]], v_new[j]) for j in nc]
        states = [states[j] * egl_ref[idx[j] * SUBLANES:idx[j] * SUBLANES + 1, :] + ds[j]
                  for j in nc]
        o = [_dot(qd_ref[bb, rows, hv[h]], s_hi[j])
             + _dot(qk_ref[bb * GDN_HEADS + h, rows, :], v_new[j])
             for j, (bb, h) in enumerate(chains)]
        for j, (bb, h) in enumerate(chains):
            o_ref[bb, rows, hv[h]] = _rms(o[j], nw) * _silu(z_ref[bb, rows, hv[h]])
    for j, (bb, h) in enumerate(chains):
        state_ref[bb * GDN_HEADS + h] = states[j]


def _gdn(x2, n_pre, wa, ws, cw, alog_row, dt_row, nw, layer, batch, seq,
         nb=2, ts=512, solve_group=8):
    ns = seq // ts
    n_chunks = ts // CHUNK
    n_inst = nb * n_chunks * GDN_HEADS
    x3 = x2.reshape(batch, seq, D_MODEL)
    out = pl.pallas_call(
        functools.partial(_gdn_kernel, nb=nb, ts=ts, solve_group=solve_group),
        grid=(batch // nb, ns),
        in_specs=[pl.BlockSpec((nb, ts, D_MODEL), lambda b, s: (b, s, 0)),
                  _const_spec((1, D_MODEL)),
                  _layer_spec((D_MODEL, GDN_COLS), layer),
                  _layer_spec((D_MODEL, LANES), layer),
                  _const_spec((CONV_WIDTH, GDN_CONV_COLS)),
                  _const_spec((1, LANES)),
                  _const_spec((1, LANES)),
                  _const_spec((1, GDN_DV))],
        out_specs=pl.BlockSpec((nb, ts, GDN_V), lambda b, s: (b, s, 0)),
        out_shape=jax.ShapeDtypeStruct((batch, seq, GDN_V), F32),
        scratch_shapes=[pltpu.VMEM((nb * GDN_HEADS, GDN_DK, GDN_DV), F32),
                        pltpu.VMEM((nb, CARRY_ROWS, GDN_CONV_COLS), F32),
                        pltpu.VMEM((nb, ts, D_MODEL), BF16),
                        pltpu.VMEM((nb, ts, GDN_CONV_COLS), F32),
                        pltpu.VMEM((nb, ts, GDN_V), F32),
                        pltpu.VMEM((nb, ts, LANES), F32),
                        pltpu.VMEM((nb, ts, LANES), F32),
                        pltpu.VMEM((nb * n_chunks, CHUNK, GDN_HEADS * CHUNK), BF16),
                        pltpu.VMEM((n_inst, 2 * CHUNK, GDN_DK), BF16),
                        pltpu.VMEM((nb, ts, GDN_QK), BF16),
                        pltpu.VMEM((n_inst, GDN_DK, CHUNK), BF16),
                        pltpu.VMEM((nb * GDN_HEADS, ts, CHUNK), BF16),
                        pltpu.VMEM((n_inst * SUBLANES, LANES), F32)],
        compiler_params=pltpu.CompilerParams(
            dimension_semantics=("arbitrary", "arbitrary"), vmem_limit_bytes=VMEM_LIMIT),
        name="gdn_mixer",
    )(x3, n_pre, wa, ws, cw, alog_row, dt_row, nw)
    return out.reshape(batch * seq, GDN_V)


PB_Q, PB_K, PB_V, PB_R = 0, GLA_QK, 2 * GLA_QK, 2 * GLA_QK + GLA_V


def _gla_kernel(x_ref, npre_ref, wb_ref, ws_ref, w2_ref, gb_ref, nw_ref, o_ref,
                state_ref, h_ref, pb_ref, b_ref, *, ts):
    n_chunks = ts // CHUNK

    @pl.when(pl.program_id(1) == 0)
    def _():
        state_ref[...] = jnp.zeros_like(state_ref)

    h_ref[...] = _bf(_rms(x_ref[...], npre_ref[...]))

    def project(col, width):
        pb_ref[:, col:col + width] = _dot(h_ref[...], wb_ref[:, col:col + width])

    sm = _dot(h_ref[...], ws_ref[...])
    logit = _dot_x3(sm, w2_ref[...]) + gb_ref[...]
    log_a = -_softplus(-logit) * (1.0 / GLA_TAU)
    causal, _ = _tri_masks()
    tril = _bf(jnp.where(causal, 1.0, 0.0))
    rows = [slice(c * CHUNK, (c + 1) * CHUNK) for c in range(n_chunks)]
    for r in rows:
        b_ref[r, :] = _dot_exact_lhs(tril, log_a[r, :])

    project(PB_Q, 2 * GLA_QK)
    project(PB_V, GLA_DV)
    project(PB_R, GLA_DV)
    nw = nw_ref[...]
    cs = range(n_chunks)
    for h in range(GLA_HEADS):
        kc = slice(h * GLA_DK, (h + 1) * GLA_DK)
        vcol = PB_V + h * GLA_DV
        rcol = PB_R + h * GLA_DV
        b = [b_ref[r, kc] for r in rows]
        q = [pb_ref[r, PB_Q + h * GLA_DK:PB_Q + (h + 1) * GLA_DK] * (GLA_DK ** -0.5) for r in rows]
        k = [pb_ref[r, PB_K + h * GLA_DK:PB_K + (h + 1) * GLA_DK] for r in rows]
        v = [pb_ref[r, vcol:vcol + GLA_DV] for r in rows]
        b_last = [x[CHUNK - 1:CHUNK, :] for x in b]
        b_mid = [x[CHUNK // 2:CHUNK // 2 + 1, :] for x in b]
        attn = [_dot_nt(_bf(q[c] * jnp.exp(b[c] - b_mid[c])), _bf(k[c] * jnp.exp(b_mid[c] - b[c])))
                for c in cs]
        if h + 1 < GLA_HEADS:
            project(vcol + GLA_DV, GLA_DV)
        d_state = [_dot(_bf(v[c].T), _bf(k[c] * jnp.exp(b_last[c] - b[c]))) for c in cs]
        o_intra = [_dot(_bf(jnp.where(causal, attn[c], 0.0)), _bf(v[c])) for c in cs]
        if h + 1 < GLA_HEADS:
            project(rcol + GLA_DV, GLA_DV)
        state = state_ref[h]
        o_inter = []
        for c in cs:
            o_inter.append(_dot_nt(_bf(q[c] * jnp.exp(b[c])), _bf(state)))
            state = state * jnp.exp(b_last[c]) + d_state[c]
        state_ref[h] = state
        for c in cs:
            o_ref[rows[c], h * GLA_DV:(h + 1) * GLA_DV] = (
                _rms(o_intra[c] + o_inter[c], nw) * _silu(pb_ref[rows[c], rcol:rcol + GLA_DV]))


def _gla(x2, n_pre, wb, ws, w2pad, gate_b, nw, layer, batch, seq, ts=512):
    ns = seq // ts
    row = lambda b, s: (b * ns + s, 0)
    return pl.pallas_call(
        functools.partial(_gla_kernel, ts=ts),
        grid=(batch, ns),
        in_specs=[pl.BlockSpec((ts, D_MODEL), row),
                  _const_spec((1, D_MODEL)),
                  _layer_spec((D_MODEL, GLA_COLS), layer),
                  _layer_spec((D_MODEL, LANES), layer),
                  _const_spec((LANES, GLA_QK)),
                  _const_spec((1, GLA_QK)),
                  _const_spec((1, GLA_DV))],
        out_specs=pl.BlockSpec((ts, GLA_V), row),
        out_shape=jax.ShapeDtypeStruct((batch * seq, GLA_V), F32),
        scratch_shapes=[pltpu.VMEM((GLA_HEADS, GLA_DV, GLA_DK), F32),
                        pltpu.VMEM((ts, D_MODEL), BF16),
                        pltpu.VMEM((ts, GLA_COLS), F32),
                        pltpu.VMEM((ts, GLA_QK), F32)],
        compiler_params=pltpu.CompilerParams(
            dimension_semantics=("arbitrary", "arbitrary"), vmem_limit_bytes=VMEM_LIMIT),
        name="gla_mixer",
    )(x2, n_pre, wb, ws, w2pad, gate_b, nw)


def _merge_mlp_kernel(ya_ref, yb_ref, x_ref, n_mix_ref, wg_ref, woa_ref, wob_ref, wo_ref,
                      n_post_ref, n_pre_ref, wup_ref, wdn_ref, n_mlp_ref, o_ref):
    x = x_ref[...]
    hg = _bf(_rms(x, n_mix_ref[...]))
    ya = _dot(_bf(ya_ref[...]), woa_ref[...])
    merged = _sigmoid(_dot(hg, wg_ref[:, :D_MODEL])) * ya
    yb = _dot(_bf(yb_ref[...]), wob_ref[...])
    merged = merged + _sigmoid(_dot(hg, wg_ref[:, D_MODEL:])) * yb
    y = _dot(_bf(merged), wo_ref[...])
    x1 = x + _rms(y, n_post_ref[...])
    h = _bf(_rms(x1, n_pre_ref[...]))
    acc = jnp.zeros(x1.shape, F32)
    for c in range(D_FF // D_MODEL):
        cs = slice(c * D_MODEL, (c + 1) * D_MODEL)
        u = jnp.square(jnp.maximum(_dot(h, wup_ref[:, cs]), 0.0))
        acc = acc + _dot(_bf(u), wdn_ref[cs, :])
    o_ref[...] = x1 + _rms(acc, n_mlp_ref[...])


def _merge_mlp(ya, yb, x2, n_mix, wg, woa, wob, wo, n_post, n_pre, wup, wdn, n_mlp, layer, tm=512):
    t = x2.shape[0]
    row = lambda i: (i, 0)
    return pl.pallas_call(
        _merge_mlp_kernel,
        grid=(t // tm,),
        in_specs=[pl.BlockSpec((tm, GDN_V), row),
                  pl.BlockSpec((tm, GLA_V), row),
                  pl.BlockSpec((tm, D_MODEL), row),
                  _const_spec((1, D_MODEL)),
                  _layer_spec((D_MODEL, GATE_COLS), layer),
                  _layer_spec((GDN_V, D_MODEL), layer),
                  _layer_spec((GLA_V, D_MODEL), layer),
                  _layer_spec((D_MODEL, D_MODEL), layer),
                  _const_spec((1, D_MODEL)),
                  _const_spec((1, D_MODEL)),
                  _layer_spec((D_MODEL, D_FF), layer),
                  _layer_spec((D_FF, D_MODEL), layer),
                  _const_spec((1, D_MODEL))],
        out_specs=pl.BlockSpec((tm, D_MODEL), row),
        out_shape=jax.ShapeDtypeStruct((t, D_MODEL), F32),
        compiler_params=pltpu.CompilerParams(
            dimension_semantics=("arbitrary",), vmem_limit_bytes=VMEM_LIMIT),
        name="merge_mlp",
    )(ya, yb, x2, n_mix, wg, woa, wob, wo, n_post, n_pre, wup, wdn, n_mlp)


def _pad_row(v, offset):
    return jnp.zeros((1, LANES), F32).at[0, offset:offset + v.shape[0]].set(v)


def kernel(x, w_in, conv_w, a_log, dt_bias, gdn_norm, gla_gate_w2, gla_gate_b, gla_norm,
           w_out_a, w_out_b, w_o, norm_mix_pre, norm_mix_post, norm_mlp_pre, norm_mlp_post,
           w_mlp_up, w_mlp_down):
    batch, seq, d = x.shape
    x2 = x.reshape(batch * seq, d)
    w_a = w_in[:, :, OFF_GDN:OFF_GDN + GDN_COLS].astype(BF16)
    w_b = w_in[:, :, OFF_GLA:OFF_GLA + GLA_COLS].astype(BF16)
    w_g = w_in[:, :, OFF_GATE:OFF_GATE + GATE_COLS].astype(BF16)
    n_small = 2 * GDN_HEADS + GLA_GATE_RANK
    w_s = jnp.concatenate(
        [w_in[:, :, OFF_BETA:OFF_BETA + 2 * GDN_HEADS], w_in[:, :, OFF_GLR:OFF_GLR + GLA_GATE_RANK],
         jnp.zeros((DEPTH, d, LANES - n_small), w_in.dtype)], axis=-1).astype(BF16)
    woa_b, wob_b, wo_b = w_out_a.astype(BF16), w_out_b.astype(BF16), w_o.astype(BF16)
    wup_b, wdn_b = w_mlp_up.astype(BF16), w_mlp_down.astype(BF16)
    w2pad = jnp.zeros((DEPTH, LANES, GLA_QK), F32).at[:, SM_GLR:SM_GLR + GLA_GATE_RANK].set(
        gla_gate_w2)
    for l in range(DEPTH):
        n_mix = norm_mix_pre[l][None, :]
        ya = _gdn(x2, n_mix, w_a, w_s, conv_w[l], _pad_row(a_log[l], SM_DECAY),
                  _pad_row(dt_bias[l], SM_DECAY), gdn_norm[l][None, :], l, batch, seq)
        yb = _gla(x2, n_mix, w_b, w_s, w2pad[l], gla_gate_b[l][None, :], gla_norm[l][None, :],
                  l, batch, seq)
        x2 = _merge_mlp(ya, yb, x2, n_mix, w_g, woa_b, wob_b, wo_b,
                        norm_mix_post[l][None, :], norm_mlp_pre[l][None, :],
                        wup_b, wdn_b, norm_mlp_post[l][None, :], l)
    return x2.reshape(batch, seq, d)
```

```python
import functools

import jax
import jax.numpy as jnp
from jax import lax
from jax.experimental import pallas as pl
from jax.experimental.pallas import tpu as pltpu

D_MODEL = 1024
DEPTH = 4
GDN_HEADS = 4
GDN_DK = 128
GDN_DV = 128
CONV_WIDTH = 4
GLA_HEADS = 4
GLA_DK = 128
GLA_DV = 256
GLA_GATE_RANK = 16
GLA_TAU = 16.0
CHUNK = 64
D_FF = 4 * D_MODEL
EPS = 1e-6

GDN_QK = GDN_HEADS * GDN_DK
GDN_V = GDN_HEADS * GDN_DV
GLA_QK = GLA_HEADS * GLA_DK
GLA_V = GLA_HEADS * GLA_DV

LANES = 128
SUBLANES = 8
CARRY_ROWS = SUBLANES

GDN_COLS = 2 * GDN_QK + 2 * GDN_V
GLA_COLS = 2 * GLA_QK + 2 * GLA_V
GATE_COLS = 2 * D_MODEL
OFF_GDN = 0
OFF_BETA = OFF_GDN + GDN_COLS
OFF_GLA = OFF_BETA + 2 * GDN_HEADS
OFF_GLR = OFF_GLA + GLA_COLS
OFF_GATE = OFF_GLR + GLA_GATE_RANK
GDN_CONV_COLS = 2 * GDN_QK + GDN_V
SM_BETA = 0
SM_DECAY = GDN_HEADS
SM_GLR = 2 * GDN_HEADS

VMEM_LIMIT = 56 * 1024 * 1024

F32 = jnp.float32
BF16 = jnp.bfloat16


def _dot(a, b):
    return jnp.dot(a, b, preferred_element_type=F32)


def _dot_nt(a, b):
    return lax.dot_general(a, b, (((1,), (1,)), ((), ())), preferred_element_type=F32)


def _bf(x):
    return x.astype(BF16)


def _split3(x):
    hi = _bf(x)
    r = x - hi.astype(F32)
    mid = _bf(r)
    lo = _bf(r - mid.astype(F32))
    return hi, mid, lo


def _dot_exact_lhs(a_bf, x):
    n = x.shape[1]
    y = _dot(a_bf, jnp.concatenate(_split3(x), axis=1))
    return (y[:, :n] + y[:, n:2 * n]) + y[:, 2 * n:]


def _dot_x3(a, b):
    a_hi = _bf(a)
    a_lo = _bf(a - a_hi.astype(F32))
    b_hi = _bf(b)
    b_lo = _bf(b - b_hi.astype(F32))
    return (_dot(a_hi, b_lo) + _dot(a_lo, b_hi)) + _dot(a_hi, b_hi)


def _rms(x, w):
    return x * lax.rsqrt(jnp.mean(x * x, axis=-1, keepdims=True) + EPS) * w


def _sigmoid(x):
    return 1.0 / (1.0 + jnp.exp(-x))


def _silu(x):
    return x * _sigmoid(x)


def _softplus(x):
    return jnp.maximum(x, 0.0) + jnp.log(1.0 + jnp.exp(-jnp.abs(x)))


def _const_spec(shape):
    nd = len(shape)
    return pl.BlockSpec(shape, lambda *_: (0,) * nd, pipeline_mode=pl.Buffered(1))


def _layer_spec(shape, layer):
    nd = len(shape)
    return pl.BlockSpec((None,) + tuple(shape), lambda *_: (layer,) + (0,) * nd,
                        pipeline_mode=pl.Buffered(1))


def _tri_masks():
    row = lax.broadcasted_iota(jnp.int32, (CHUNK, CHUNK), 0)
    col = lax.broadcasted_iota(jnp.int32, (CHUNK, CHUNK), 1)
    return row >= col, row > col


def _gdn_kernel(x_ref, npre_ref, wa_ref, ws_ref, cw_ref, alog_ref, dt_ref, nw_ref, o_ref,
                state_ref, carry_ref, h_ref, act_ref, z_ref, beta_ref, g_ref,
                tinv_ref, kg_ref, qd_ref, kdt_ref, qk_ref, egl_ref, *, nb, ts, solve_group):
    n_chunks = ts // CHUNK

    @pl.when(pl.program_id(1) == 0)
    def _():
        state_ref[...] = jnp.zeros_like(state_ref)
        carry_ref[...] = jnp.zeros_like(carry_ref)

    tile = 2 * LANES
    for bb in range(nb):
        h_ref[bb] = _bf(_rms(x_ref[bb], npre_ref[...]))
    for bb in range(nb):
        sm = _dot(h_ref[bb], ws_ref[...])
        beta_ref[bb] = _sigmoid(sm)
        g_ref[bb] = -(jnp.exp(alog_ref[...]) * _softplus(sm + dt_ref[...]))
        for t in range(GDN_COLS // tile):
            proj = _dot(h_ref[bb], wa_ref[:, t * tile:(t + 1) * tile])
            if t * tile >= GDN_CONV_COLS:
                z_ref[bb, :, t * tile - GDN_CONV_COLS:(t + 1) * tile - GDN_CONV_COLS] = proj
                continue
            for j in range(t * tile // LANES, (t + 1) * tile // LANES):
                cs = slice(j * LANES, (j + 1) * LANES)
                x = proj[:, j * LANES - t * tile:(j + 1) * LANES - t * tile]
                xx = jnp.concatenate([carry_ref[bb, :, cs], x], axis=0)
                base = CARRY_ROWS - (CONV_WIDTH - 1)
                acc = xx[base:base + ts] * cw_ref[0:1, cs]
                for k in range(1, CONV_WIDTH):
                    acc = acc + xx[base + k:base + k + ts] * cw_ref[k:k + 1, cs]
                carry_ref[bb, :, cs] = x[ts - CARRY_ROWS:ts]
                a = _silu(acc)
                if j < 2 * GDN_HEADS:
                    a = a * lax.rsqrt(jnp.sum(a * a, axis=-1, keepdims=True) + EPS)
                    if j < GDN_HEADS:
                        a = a * (GDN_DK ** -0.5)
                act_ref[bb, :, cs] = a

    causal, strict = _tri_masks()
    tril = _bf(jnp.where(causal, 1.0, 0.0))
    eye = jnp.where(causal & jnp.logical_not(strict), 1.0, 0.0).astype(F32)
    rrow = lax.broadcasted_iota(jnp.int32, (CHUNK, 4 * CHUNK), 0)
    rcol = lax.broadcasted_iota(jnp.int32, (CHUNK, 4 * CHUNK), 1)
    rhs_mask = ((rcol < CHUNK) & (rrow > rcol)) | (rcol >= 2 * CHUNK)
    hv = [slice(h * GDN_DV, (h + 1) * GDN_DV) for h in range(GDN_HEADS)]
    packed = GDN_HEADS * CHUNK
    prow = lax.broadcasted_iota(jnp.int32, (packed, packed), 0)
    pcol = lax.broadcasted_iota(jnp.int32, (packed, packed), 1)
    blk_mask = (prow // CHUNK) == (pcol // CHUNK)
    eye_p = jnp.concatenate([eye] * GDN_HEADS, axis=1)

    def solve_body(grp, carry):
        insts = [(bb, ci, h) for bb in range(nb) for ci in range(solve_group)
                 for h in range(GDN_HEADS)]
        n = len(insts)
        chunk_ids = [grp * solve_group + ci for ci in range(solve_group)]
        rows = [pl.ds(pl.multiple_of(c * CHUNK, CHUNK), CHUNK) for c in chunk_ids]
        beta_c = {(bb, ci): beta_ref[bb, rows[ci], :] for bb in range(nb)
                  for ci in range(solve_group)}
        g_c = {(bb, ci): g_ref[bb, rows[ci], :] for bb in range(nb) for ci in range(solve_group)}

        dg = [_dot_exact_lhs(tril, jnp.where(
            rhs_mask, g_c[bb, ci][:, SM_DECAY + h:SM_DECAY + h + 1], 0.0)) for bb, ci, h in insts]
        q = [act_ref[bb, rows[ci], h * GDN_DK:(h + 1) * GDN_DK] for bb, ci, h in insts]
        k = [act_ref[bb, rows[ci], GDN_QK + h * GDN_DK:GDN_QK + (h + 1) * GDN_DK]
             for bb, ci, h in insts]
        kb = [k[i] * beta_c[bb, ci][:, SM_BETA + h:SM_BETA + h + 1]
              for i, (bb, ci, h) in enumerate(insts)]
        kq = [_dot_nt(_bf(jnp.concatenate([kb[i], q[i]], axis=0)), _bf(k[i])) for i in range(n)]
        decay = [jnp.where(causal, jnp.exp(d[:, :CHUNK]), 0.0) for d in dg]
        gc = [d[:, 2 * CHUNK:] for d in dg]
        egc = [jnp.exp(x) for x in gc]
        a_mat = [jnp.where(strict, kq[i][:CHUNK] * decay[i], 0.0) for i in range(n)]
        groups = [(bb, ci) for bb in range(nb) for ci in range(solve_group)]
        ng = range(len(groups))
        a_p = [jnp.concatenate([a_mat[gi * GDN_HEADS + h] for h in range(GDN_HEADS)], axis=1)
               for gi in ng]

        def block_diag(x_bf):
            return jnp.where(blk_mask, jnp.concatenate([x_bf] * GDN_HEADS, axis=0),
                             jnp.zeros((), BF16))

        p = [_bf(x) for x in a_p]
        pd = [block_diag(x) for x in p]
        tinv = [eye_p - x for x in a_p]
        power = 2
        while True:
            pp = [_dot(p[gi], pd[gi]) for gi in ng]
            p = [_bf(x) for x in pp]
            pd = [block_diag(x) for x in p]
            xp = [_dot(_bf(tinv[gi]), pd[gi]) for gi in ng]
            tinv = [tinv[gi] + xp[gi] for gi in ng]
            power *= 2
            if power >= CHUNK:
                break
        for gi, (bb, ci) in enumerate(groups):
            tinv_ref[bb * n_chunks + chunk_ids[ci]] = _bf(tinv[gi])
        for i, (bb, ci, h) in enumerate(insts):
            idx = (bb * n_chunks + chunk_ids[ci]) * GDN_HEADS + h
            gl = gc[i][CHUNK - 1:CHUNK, :]
            kg = k[i] * egc[i]
            kg_hi = _bf(kg)
            kg_ref[idx, :CHUNK, :] = kg_hi
            kg_ref[idx, CHUNK:, :] = _bf(kg - kg_hi.astype(F32))
            qd_ref[bb, rows[ci], hv[h]] = _bf(q[i] * egc[i])
            kdt_ref[idx] = _bf((k[i] * jnp.exp(gl - gc[i])).T)
            qk_ref[bb * GDN_HEADS + h, rows[ci], :] = _bf(kq[i][CHUNK:] * decay[i])
            egl_ref[pl.ds(pl.multiple_of(idx * SUBLANES, SUBLANES), SUBLANES), :] = (
                jnp.broadcast_to(jnp.exp(gl), (SUBLANES, LANES)))
        return carry

    lax.fori_loop(0, n_chunks // solve_group, solve_body, 0)

    nw = nw_ref[...]
    chains = [(bb, h) for bb in range(nb) for h in range(GDN_HEADS)]
    nc = range(len(chains))
    states = [state_ref[bb * GDN_HEADS + h] for bb, h in chains]
    for c in range(n_chunks):
        rows = slice(c * CHUNK, (c + 1) * CHUNK)
        idx = [(bb * n_chunks + c) * GDN_HEADS + h for bb, h in chains]
        s_hi = [_bf(states[j]) for j in nc]
        s_lo = [_bf(states[j] - s_hi[j].astype(F32)) for j in nc]
        ks_a = [_dot(kg_ref[idx[j]], s_hi[j]) for j in nc]
        ks_b = [_dot(kg_ref[idx[j], :CHUNK, :], s_lo[j]) for j in nc]
        resid = [(act_ref[bb, rows, 2 * GDN_QK + h * GDN_DV:2 * GDN_QK + (h + 1) * GDN_DV]
                  - ((ks_b[j] + ks_a[j][CHUNK:]) + ks_a[j][:CHUNK]))
                 * beta_ref[bb, rows, SM_BETA + h:SM_BETA + h + 1]
                 for j, (bb, h) in enumerate(chains)]
        zero = jnp.zeros((CHUNK, GDN_DV), BF16)

        def block_diag_heads(xs):
            return jnp.concatenate(
                [jnp.concatenate([xs[h] if hh == h else zero for hh in range(GDN_HEADS)], axis=1)
                 for h in range(GDN_HEADS)], axis=0)

        v_new = []
        for bb in range(nb):
            r_bf = [_bf(resid[bb * GDN_HEADS + h]) for h in range(GDN_HEADS)]
            v_all = _dot(tinv_ref[bb * n_chunks + c], block_diag_heads(r_bf))
            v_new += [_bf(v_all[:, hv[h]]) for h in range(GDN_HEADS)]
        ds = [_dot(kdt_ref[idx[j]], v_new[j]) for j in nc]
        states = [states[j] * egl_ref[idx[j] * SUBLANES:idx[j] * SUBLANES + 1, :] + ds[j]
                  for j in nc]
        o = [_dot(qd_ref[bb, rows, hv[h]], s_hi[j])
             + _dot(qk_ref[bb * GDN_HEADS + h, rows, :], v_new[j])
             for j, (bb, h) in enumerate(chains)]
        for j, (bb, h) in enumerate(chains):
            o_ref[bb, rows, hv[h]] = _rms(o[j], nw) * _silu(z_ref[bb, rows, hv[h]])
    for j, (bb, h) in enumerate(chains):
        state_ref[bb * GDN_HEADS + h] = states[j]


def _gdn(x2, n_pre, wa, ws, cw, alog_row, dt_row, nw, layer, batch, seq,
         nb=2, ts=512, solve_group=8):
    ns = seq // ts
    n_chunks = ts // CHUNK
    n_inst = nb * n_chunks * GDN_HEADS
    x3 = x2.reshape(batch, seq, D_MODEL)
    out = pl.pallas_call(
        functools.partial(_gdn_kernel, nb=nb, ts=ts, solve_group=solve_group),
        grid=(batch // nb, ns),
        in_specs=[pl.BlockSpec((nb, ts, D_MODEL), lambda b, s: (b, s, 0)),
                  _const_spec((1, D_MODEL)),
                  _layer_spec((D_MODEL, GDN_COLS), layer),
                  _layer_spec((D_MODEL, LANES), layer),
                  _const_spec((CONV_WIDTH, GDN_CONV_COLS)),
                  _const_spec((1, LANES)),
                  _const_spec((1, LANES)),
                  _const_spec((1, GDN_DV))],
        out_specs=pl.BlockSpec((nb, ts, GDN_V), lambda b, s: (b, s, 0)),
        out_shape=jax.ShapeDtypeStruct((batch, seq, GDN_V), F32),
        scratch_shapes=[pltpu.VMEM((nb * GDN_HEADS, GDN_DK, GDN_DV), F32),
                        pltpu.VMEM((nb, CARRY_ROWS, GDN_CONV_COLS), F32),
                        pltpu.VMEM((nb, ts, D_MODEL), BF16),
                        pltpu.VMEM((nb, ts, GDN_CONV_COLS), F32),
                        pltpu.VMEM((nb, ts, GDN_V), F32),
                        pltpu.VMEM((nb, ts, LANES), F32),
                        pltpu.VMEM((nb, ts, LANES), F32),
                        pltpu.VMEM((nb * n_chunks, CHUNK, GDN_HEADS * CHUNK), BF16),
                        pltpu.VMEM((n_inst, 2 * CHUNK, GDN_DK), BF16),
                        pltpu.VMEM((nb, ts, GDN_QK), BF16),
                        pltpu.VMEM((n_inst, GDN_DK, CHUNK), BF16),
                        pltpu.VMEM((nb * GDN_HEADS, ts, CHUNK), BF16),
                        pltpu.VMEM((n_inst * SUBLANES, LANES), F32)],
        compiler_params=pltpu.CompilerParams(
            dimension_semantics=("arbitrary", "arbitrary"), vmem_limit_bytes=VMEM_LIMIT),
        name="gdn_mixer",
    )(x3, n_pre, wa, ws, cw, alog_row, dt_row, nw)
    return out.reshape(batch * seq, GDN_V)


PB_Q, PB_K, PB_V, PB_R = 0, GLA_QK, 2 * GLA_QK, 2 * GLA_QK + GLA_V


def _gla_kernel(x_ref, npre_ref, wb_ref, ws_ref, w2_ref, gb_ref, nw_ref, o_ref,
                state_ref, h_ref, pb_ref, b_ref, *, ts):
    n_chunks = ts // CHUNK

    @pl.when(pl.program_id(1) == 0)
    def _():
        state_ref[...] = jnp.zeros_like(state_ref)

    h_ref[...] = _bf(_rms(x_ref[...], npre_ref[...]))

    def project(col, width):
        pb_ref[:, col:col + width] = _dot(h_ref[...], wb_ref[:, col:col + width])

    sm = _dot(h_ref[...], ws_ref[...])
    logit = _dot_x3(sm, w2_ref[...]) + gb_ref[...]
    log_a = -_softplus(-logit) * (1.0 / GLA_TAU)
    causal, _ = _tri_masks()
    tril = _bf(jnp.where(causal, 1.0, 0.0))
    rows = [slice(c * CHUNK, (c + 1) * CHUNK) for c in range(n_chunks)]
    for r in rows:
        b_ref[r, :] = _dot_exact_lhs(tril, log_a[r, :])

    project(PB_Q, 2 * GLA_QK)
    project(PB_V, GLA_DV)
    project(PB_R, GLA_DV)
    nw = nw_ref[...]
    cs = range(n_chunks)
    for h in range(GLA_HEADS):
        kc = slice(h * GLA_DK, (h + 1) * GLA_DK)
        vcol = PB_V + h * GLA_DV
        rcol = PB_R + h * GLA_DV
        b = [b_ref[r, kc] for r in rows]
        q = [pb_ref[r, PB_Q + h * GLA_DK:PB_Q + (h + 1) * GLA_DK] * (GLA_DK ** -0.5) for r in rows]
        k = [pb_ref[r, PB_K + h * GLA_DK:PB_K + (h + 1) * GLA_DK] for r in rows]
        v = [pb_ref[r, vcol:vcol + GLA_DV] for r in rows]
        b_last = [x[CHUNK - 1:CHUNK, :] for x in b]
        b_mid = [x[CHUNK // 2:CHUNK // 2 + 1, :] for x in b]
        attn = [_dot_nt(_bf(q[c] * jnp.exp(b[c] - b_mid[c])), _bf(k[c] * jnp.exp(b_mid[c] - b[c])))
                for c in cs]
        if h + 1 < GLA_HEADS:
            project(vcol + GLA_DV, GLA_DV)
        d_state = [_dot(_bf(v[c].T), _bf(k[c] * jnp.exp(b_last[c] - b[c]))) for c in cs]
        o_intra = [_dot(_bf(jnp.where(causal, attn[c], 0.0)), _bf(v[c])) for c in cs]
        if h + 1 < GLA_HEADS:
            project(rcol + GLA_DV, GLA_DV)
        state = state_ref[h]
        o_inter = []
        for c in cs:
            o_inter.append(_dot_nt(_bf(q[c] * jnp.exp(b[c])), _bf(state)))
            state = state * jnp.exp(b_last[c]) + d_state[c]
        state_ref[h] = state
        for c in cs:
            o_ref[rows[c], h * GLA_DV:(h + 1) * GLA_DV] = (
                _rms(o_intra[c] + o_inter[c], nw) * _silu(pb_ref[rows[c], rcol:rcol + GLA_DV]))


def _gla(x2, n_pre, wb, ws, w2pad, gate_b, nw, layer, batch, seq, ts=512):
    ns = seq // ts
    row = lambda b, s: (b * ns + s, 0)
    return pl.pallas_call(
        functools.partial(_gla_kernel, ts=ts),
        grid=(batch, ns),
        in_specs=[pl.BlockSpec((ts, D_MODEL), row),
                  _const_spec((1, D_MODEL)),
                  _layer_spec((D_MODEL, GLA_COLS), layer),
                  _layer_spec((D_MODEL, LANES), layer),
                  _const_spec((LANES, GLA_QK)),
                  _const_spec((1, GLA_QK)),
                  _const_spec((1, GLA_DV))],
        out_specs=pl.BlockSpec((ts, GLA_V), row),
        out_shape=jax.ShapeDtypeStruct((batch * seq, GLA_V), F32),
        scratch_shapes=[pltpu.VMEM((GLA_HEADS, GLA_DV, GLA_DK), F32),
                        pltpu.VMEM((ts, D_MODEL), BF16),
                        pltpu.VMEM((ts, GLA_COLS), F32),
                        pltpu.VMEM((ts, GLA_QK), F32)],
        compiler_params=pltpu.CompilerParams(
            dimension_semantics=("arbitrary", "arbitrary"), vmem_limit_bytes=VMEM_LIMIT),
        name="gla_mixer",
    )(x2, n_pre, wb, ws, w2pad, gate_b, nw)


def _merge_mlp_kernel(ya_ref, yb_ref, x_ref, n_mix_ref, wg_ref, woa_ref, wob_ref, wo_ref,
                      n_post_ref, n_pre_ref, wup_ref, wdn_ref, n_mlp_ref, o_ref):
    x = x_ref[...]
    hg = _bf(_rms(x, n_mix_ref[...]))
    ya = _dot(_bf(ya_ref[...]), woa_ref[...])
    merged = _sigmoid(_dot(hg, wg_ref[:, :D_MODEL])) * ya
    yb = _dot(_bf(yb_ref[...]), wob_ref[...])
    merged = merged + _sigmoid(_dot(hg, wg_ref[:, D_MODEL:])) * yb
    y = _dot(_bf(merged), wo_ref[...])
    x1 = x + _rms(y, n_post_ref[...])
    h = _bf(_rms(x1, n_pre_ref[...]))
    acc = jnp.zeros(x1.shape, F32)
    for c in range(D_FF // D_MODEL):
        cs = slice(c * D_MODEL, (c + 1) * D_MODEL)
        u = jnp.square(jnp.maximum(_dot(h, wup_ref[:, cs]), 0.0))
        acc = acc + _dot(_bf(u), wdn_ref[cs, :])
    o_ref[...] = x1 + _rms(acc, n_mlp_ref[...])


def _merge_mlp(ya, yb, x2, n_mix, wg, woa, wob, wo, n_post, n_pre, wup, wdn, n_mlp, layer, tm=512):
    t = x2.shape[0]
    row = lambda i: (i, 0)
    return pl.pallas_call(
        _merge_mlp_kernel,
        grid=(t // tm,),
        in_specs=[pl.BlockSpec((tm, GDN_V), row),
                  pl.BlockSpec((tm, GLA_V), row),
                  pl.BlockSpec((tm, D_MODEL), row),
                  _const_spec((1, D_MODEL)),
                  _layer_spec((D_MODEL, GATE_COLS), layer),
                  _layer_spec((GDN_V, D_MODEL), layer),
                  _layer_spec((GLA_V, D_MODEL), layer),
                  _layer_spec((D_MODEL, D_MODEL), layer),
                  _const_spec((1, D_MODEL)),
                  _const_spec((1, D_MODEL)),
                  _layer_spec((D_MODEL, D_FF), layer),
                  _layer_spec((D_FF, D_MODEL), layer),
                  _const_spec((1, D_MODEL))],
        out_specs=pl.BlockSpec((tm, D_MODEL), row),
        out_shape=jax.ShapeDtypeStruct((t, D_MODEL), F32),
        compiler_params=pltpu.CompilerParams(
            dimension_semantics=("arbitrary",), vmem_limit_bytes=VMEM_LIMIT),
        name="merge_mlp",
    )(ya, yb, x2, n_mix, wg, woa, wob, wo, n_post, n_pre, wup, wdn, n_mlp)


def _split_w_in_kernel(w_ref, oa_ref, ob_ref, og_ref, os_ref):
    oa_ref[...] = _bf(w_ref[:, OFF_GDN:OFF_GDN + GDN_COLS])
    ob_ref[...] = _bf(w_ref[:, OFF_GLA:OFF_GLA + GLA_COLS])
    og_ref[...] = _bf(w_ref[:, OFF_GATE:OFF_GATE + GATE_COLS])
    n_small = 2 * GDN_HEADS + GLA_GATE_RANK
    os_ref[...] = _bf(jnp.concatenate(
        [w_ref[:, OFF_BETA:OFF_BETA + 2 * GDN_HEADS], w_ref[:, OFF_GLR:OFF_GLR + GLA_GATE_RANK],
         jnp.zeros((w_ref.shape[0], LANES - n_small), F32)], axis=1))


def _split_w_in(w_in, rows=128):
    depth, d, cols = w_in.shape
    widths = (GDN_COLS, GLA_COLS, GATE_COLS, LANES)
    blk = lambda w: pl.BlockSpec((None, rows, w), lambda l, i: (l, i, 0))
    return pl.pallas_call(
        _split_w_in_kernel,
        grid=(depth, d // rows),
        in_specs=[blk(cols)],
        out_specs=[blk(w) for w in widths],
        out_shape=[jax.ShapeDtypeStruct((depth, d, w), BF16) for w in widths],
        compiler_params=pltpu.CompilerParams(
            dimension_semantics=("arbitrary", "arbitrary"), vmem_limit_bytes=VMEM_LIMIT),
        name="split_w_in",
    )(w_in)


def _pad_row(v, offset):
    return jnp.zeros((1, LANES), F32).at[0, offset:offset + v.shape[0]].set(v)


def kernel(x, w_in, conv_w, a_log, dt_bias, gdn_norm, gla_gate_w2, gla_gate_b, gla_norm,
           w_out_a, w_out_b, w_o, norm_mix_pre, norm_mix_post, norm_mlp_pre, norm_mlp_post,
           w_mlp_up, w_mlp_down):
    batch, seq, d = x.shape
    x2 = x.reshape(batch * seq, d)
    w_a, w_b, w_g, w_s = _split_w_in(w_in)
    woa_b, wob_b, wo_b = w_out_a.astype(BF16), w_out_b.astype(BF16), w_o.astype(BF16)
    wup_b, wdn_b = w_mlp_up.astype(BF16), w_mlp_down.astype(BF16)
    w2pad = jnp.zeros((DEPTH, LANES, GLA_QK), F32).at[:, SM_GLR:SM_GLR + GLA_GATE_RANK].set(
        gla_gate_w2)
    for l in range(DEPTH):
        n_mix = norm_mix_pre[l][None, :]
        ya = _gdn(x2, n_mix, w_a, w_s, conv_w[l], _pad_row(a_log[l], SM_DECAY),
                  _pad_row(dt_bias[l], SM_DECAY), gdn_norm[l][None, :], l, batch, seq)
        yb = _gla(x2, n_mix, w_b, w_s, w2pad[l], gla_gate_b[l][None, :], gla_norm[l][None, :],
                  l, batch, seq)
        x2 = _merge_mlp(ya, yb, x2, n_mix, w_g, woa_b, wob_b, wo_b,
                        norm_mix_post[l][None, :], norm_mlp_pre[l][None, :],
                        wup_b, wdn_b, norm_mlp_post[l][None, :], l)
    return x2.reshape(batch, seq, d)
```

```python
import functools

import jax
import jax.numpy as jnp
from jax import lax
from jax.experimental import pallas as pl
from jax.experimental.pallas import tpu as pltpu

D_MODEL = 1024
DEPTH = 4
GDN_HEADS = 4
GDN_DK = 128
GDN_DV = 128
CONV_WIDTH = 4
GLA_HEADS = 4
GLA_DK = 128
GLA_DV = 256
GLA_GATE_RANK = 16
GLA_TAU = 16.0
CHUNK = 64
D_FF = 4 * D_MODEL
EPS = 1e-6

GDN_QK = GDN_HEADS * GDN_DK
GDN_V = GDN_HEADS * GDN_DV
GLA_QK = GLA_HEADS * GLA_DK
GLA_V = GLA_HEADS * GLA_DV

LANES = 128
SUBLANES = 8
CARRY_ROWS = SUBLANES

GDN_COLS = 2 * GDN_QK + 2 * GDN_V
GLA_COLS = 2 * GLA_QK + 2 * GLA_V
GATE_COLS = 2 * D_MODEL
OFF_GDN = 0
OFF_BETA = OFF_GDN + GDN_COLS
OFF_GLA = OFF_BETA + 2 * GDN_HEADS
OFF_GLR = OFF_GLA + GLA_COLS
OFF_GATE = OFF_GLR + GLA_GATE_RANK
GDN_CONV_COLS = 2 * GDN_QK + GDN_V
SM_BETA = 0
SM_DECAY = GDN_HEADS
SM_GLR = 2 * GDN_HEADS

VMEM_LIMIT = 56 * 1024 * 1024

F32 = jnp.float32
BF16 = jnp.bfloat16


def _dot(a, b):
    return jnp.dot(a, b, preferred_element_type=F32)


def _dot_nt(a, b):
    return lax.dot_general(a, b, (((1,), (1,)), ((), ())), preferred_element_type=F32)


def _bf(x):
    return x.astype(BF16)


def _dot_exact_lhs(a_bf, x):
    n = x.shape[1]
    hi = _bf(x)
    lo = _bf(x - hi.astype(F32))
    y = _dot(a_bf, jnp.concatenate([hi, lo], axis=1))
    return y[:, :n] + y[:, n:]


def _dot_x3(a, b):
    a_hi = _bf(a)
    a_lo = _bf(a - a_hi.astype(F32))
    b_hi = _bf(b)
    b_lo = _bf(b - b_hi.astype(F32))
    return (_dot(a_hi, b_lo) + _dot(a_lo, b_hi)) + _dot(a_hi, b_hi)


def _rms(x, w):
    return x * lax.rsqrt(jnp.mean(x * x, axis=-1, keepdims=True) + EPS) * w


def _sigmoid(x):
    return 1.0 / (1.0 + jnp.exp(-x))


def _silu(x):
    return x * _sigmoid(x)


def _softplus(x):
    return jnp.maximum(x, 0.0) + jnp.log(1.0 + jnp.exp(-jnp.abs(x)))


def _const_spec(shape):
    nd = len(shape)
    return pl.BlockSpec(shape, lambda *_: (0,) * nd, pipeline_mode=pl.Buffered(1))


def _layer_spec(shape, layer):
    nd = len(shape)
    return pl.BlockSpec((None,) + tuple(shape), lambda *_: (layer,) + (0,) * nd,
                        pipeline_mode=pl.Buffered(1))


def _tri_masks():
    row = lax.broadcasted_iota(jnp.int32, (CHUNK, CHUNK), 0)
    col = lax.broadcasted_iota(jnp.int32, (CHUNK, CHUNK), 1)
    return row >= col, row > col


def _gdn_kernel(x_ref, npre_ref, wa_ref, ws_ref, cw_ref, alog_ref, dt_ref, nw_ref, o_ref,
                state_ref, carry_ref, h_ref, act_ref, z_ref, beta_ref, g_ref,
                tinv_ref, kg_ref, qd_ref, kdt_ref, qk_ref, egl_ref, *, nb, ts, solve_group):
    n_chunks = ts // CHUNK

    @pl.when(pl.program_id(1) == 0)
    def _():
        state_ref[...] = jnp.zeros_like(state_ref)
        carry_ref[...] = jnp.zeros_like(carry_ref)

    tile = 2 * LANES
    for bb in range(nb):
        h_ref[bb] = _bf(_rms(x_ref[bb], npre_ref[...]))
    for bb in range(nb):
        sm = _dot(h_ref[bb], ws_ref[...])
        beta_ref[bb] = _sigmoid(sm)
        g_ref[bb] = -(jnp.exp(alog_ref[...]) * _softplus(sm + dt_ref[...]))
        for t in range(GDN_COLS // tile):
            proj = _dot(h_ref[bb], wa_ref[:, t * tile:(t + 1) * tile])
            if t * tile >= GDN_CONV_COLS:
                z_ref[bb, :, t * tile - GDN_CONV_COLS:(t + 1) * tile - GDN_CONV_COLS] = proj
                continue
            for j in range(t * tile // LANES, (t + 1) * tile // LANES):
                cs = slice(j * LANES, (j + 1) * LANES)
                x = proj[:, j * LANES - t * tile:(j + 1) * LANES - t * tile]
                xx = jnp.concatenate([carry_ref[bb, :, cs], x], axis=0)
                base = CARRY_ROWS - (CONV_WIDTH - 1)
                acc = xx[base:base + ts] * cw_ref[0:1, cs]
                for k in range(1, CONV_WIDTH):
                    acc = acc + xx[base + k:base + k + ts] * cw_ref[k:k + 1, cs]
                carry_ref[bb, :, cs] = x[ts - CARRY_ROWS:ts]
                a = _silu(acc)
                if j < 2 * GDN_HEADS:
                    a = a * lax.rsqrt(jnp.sum(a * a, axis=-1, keepdims=True) + EPS)
                    if j < GDN_HEADS:
                        a = a * (GDN_DK ** -0.5)
                act_ref[bb, :, cs] = a

    causal, strict = _tri_masks()
    tril = _bf(jnp.where(causal, 1.0, 0.0))
    eye = jnp.where(causal & jnp.logical_not(strict), 1.0, 0.0).astype(F32)
    rrow = lax.broadcasted_iota(jnp.int32, (CHUNK, 4 * CHUNK), 0)
    rcol = lax.broadcasted_iota(jnp.int32, (CHUNK, 4 * CHUNK), 1)
    rhs_mask = ((rcol < CHUNK) & (rrow > rcol)) | (rcol >= 2 * CHUNK)
    hv = [slice(h * GDN_DV, (h + 1) * GDN_DV) for h in range(GDN_HEADS)]
    packed = GDN_HEADS * CHUNK
    prow = lax.broadcasted_iota(jnp.int32, (packed, packed), 0)
    pcol = lax.broadcasted_iota(jnp.int32, (packed, packed), 1)
    blk_mask = (prow // CHUNK) == (pcol // CHUNK)
    eye_p = jnp.concatenate([eye] * GDN_HEADS, axis=1)

    def solve_body(grp, carry):
        insts = [(bb, ci, h) for bb in range(nb) for ci in range(solve_group)
                 for h in range(GDN_HEADS)]
        n = len(insts)
        chunk_ids = [grp * solve_group + ci for ci in range(solve_group)]
        rows = [pl.ds(pl.multiple_of(c * CHUNK, CHUNK), CHUNK) for c in chunk_ids]
        beta_c = {(bb, ci): beta_ref[bb, rows[ci], :] for bb in range(nb)
                  for ci in range(solve_group)}
        g_c = {(bb, ci): g_ref[bb, rows[ci], :] for bb in range(nb) for ci in range(solve_group)}

        dg = [_dot_exact_lhs(tril, jnp.where(
            rhs_mask, g_c[bb, ci][:, SM_DECAY + h:SM_DECAY + h + 1], 0.0)) for bb, ci, h in insts]
        q = [act_ref[bb, rows[ci], h * GDN_DK:(h + 1) * GDN_DK] for bb, ci, h in insts]
        k = [act_ref[bb, rows[ci], GDN_QK + h * GDN_DK:GDN_QK + (h + 1) * GDN_DK]
             for bb, ci, h in insts]
        kb = [k[i] * beta_c[bb, ci][:, SM_BETA + h:SM_BETA + h + 1]
              for i, (bb, ci, h) in enumerate(insts)]
        kq = [_dot_nt(_bf(jnp.concatenate([kb[i], q[i]], axis=0)), _bf(k[i])) for i in range(n)]
        decay = [jnp.where(causal, jnp.exp(d[:, :CHUNK]), 0.0) for d in dg]
        gc = [d[:, 2 * CHUNK:] for d in dg]
        egc = [jnp.exp(x) for x in gc]
        a_mat = [jnp.where(strict, kq[i][:CHUNK] * decay[i], 0.0) for i in range(n)]
        groups = [(bb, ci) for bb in range(nb) for ci in range(solve_group)]
        ng = range(len(groups))
        a_p = [jnp.concatenate([a_mat[gi * GDN_HEADS + h] for h in range(GDN_HEADS)], axis=1)
               for gi in ng]

        def block_diag(x_bf):
            return jnp.where(blk_mask, jnp.concatenate([x_bf] * GDN_HEADS, axis=0),
                             jnp.zeros((), BF16))

        p = [_bf(x) for x in a_p]
        pd = [block_diag(x) for x in p]
        tinv = [eye_p - x for x in a_p]
        power = 2
        while True:
            pp = [_dot(p[gi], pd[gi]) for gi in ng]
            p = [_bf(x) for x in pp]
            pd = [block_diag(x) for x in p]
            xp = [_dot(_bf(tinv[gi]), pd[gi]) for gi in ng]
            tinv = [tinv[gi] + xp[gi] for gi in ng]
            power *= 2
            if power >= CHUNK:
                break
        for gi, (bb, ci) in enumerate(groups):
            tinv_ref[bb * n_chunks + chunk_ids[ci]] = _bf(tinv[gi])
        for i, (bb, ci, h) in enumerate(insts):
            idx = (bb * n_chunks + chunk_ids[ci]) * GDN_HEADS + h
            gl = gc[i][CHUNK - 1:CHUNK, :]
            kg = k[i] * egc[i]
            kg_hi = _bf(kg)
            kg_ref[idx, :CHUNK, :] = kg_hi
            kg_ref[idx, CHUNK:, :] = _bf(kg - kg_hi.astype(F32))
            qd_ref[bb, rows[ci], hv[h]] = _bf(q[i] * egc[i])
            kdt_ref[idx] = _bf((k[i] * jnp.exp(gl - gc[i])).T)
            qk_ref[bb * GDN_HEADS + h, rows[ci], :] = _bf(kq[i][CHUNK:] * decay[i])
            egl_ref[pl.ds(pl.multiple_of(idx * SUBLANES, SUBLANES), SUBLANES), :] = (
                jnp.broadcast_to(jnp.exp(gl), (SUBLANES, LANES)))
        return carry

    lax.fori_loop(0, n_chunks // solve_group, solve_body, 0)

    nw = nw_ref[...]
    chains = [(bb, h) for bb in range(nb) for h in range(GDN_HEADS)]
    nc = range(len(chains))
    states = [state_ref[bb * GDN_HEADS + h] for bb, h in chains]
    for c in range(n_chunks):
        rows = slice(c * CHUNK, (c + 1) * CHUNK)
        idx = [(bb * n_chunks + c) * GDN_HEADS + h for bb, h in chains]
        s_hi = [_bf(states[j]) for j in nc]
        s_lo = [_bf(states[j] - s_hi[j].astype(F32)) for j in nc]
        ks_a = [_dot(kg_ref[idx[j]], s_hi[j]) for j in nc]
        ks_b = [_dot(kg_ref[idx[j], :CHUNK, :], s_lo[j]) for j in nc]
        resid = [(act_ref[bb, rows, 2 * GDN_QK + h * GDN_DV:2 * GDN_QK + (h + 1) * GDN_DV]
                  - ((ks_b[j] + ks_a[j][CHUNK:]) + ks_a[j][:CHUNK]))
                 * beta_ref[bb, rows, SM_BETA + h:SM_BETA + h + 1]
                 for j, (bb, h) in enumerate(chains)]
        zero = jnp.zeros((CHUNK, GDN_DV), BF16)

        def block_diag_heads(xs):
            return jnp.concatenate(
                [jnp.concatenate([xs[h] if hh == h else zero for hh in range(GDN_HEADS)], axis=1)
                 for h in range(GDN_HEADS)], axis=0)

        v_new = []
        for bb in range(nb):
            r_bf = [_bf(resid[bb * GDN_HEADS + h]) for h in range(GDN_HEADS)]
            v_all = _dot(tinv_ref[bb * n_chunks + c], block_diag_heads(r_bf))
            v_new += [_bf(v_all[:, hv[h]]) for h in range(GDN_HEADS)]
        ds = [_dot(kdt_ref[idx[j]], v_new[j]) for j in nc]
        states = [states[j] * egl_ref[idx[j] * SUBLANES:idx[j] * SUBLANES + 1, :] + ds[j]
                  for j in nc]
        o = [_dot(qd_ref[bb, rows, hv[h]], s_hi[j])
             + _dot(qk_ref[bb * GDN_HEADS + h, rows, :], v_new[j])
             for j, (bb, h) in enumerate(chains)]
        for j, (bb, h) in enumerate(chains):
            o_ref[bb, rows, hv[h]] = _rms(o[j], nw) * _silu(z_ref[bb, rows, hv[h]])
    for j, (bb, h) in enumerate(chains):
        state_ref[bb * GDN_HEADS + h] = states[j]


def _gdn(x2, n_pre, wa, ws, cw, alog_row, dt_row, nw, layer, batch, seq,
         nb=2, ts=512, solve_group=8):
    ns = seq // ts
    n_chunks = ts // CHUNK
    n_inst = nb * n_chunks * GDN_HEADS
    x3 = x2.reshape(batch, seq, D_MODEL)
    out = pl.pallas_call(
        functools.partial(_gdn_kernel, nb=nb, ts=ts, solve_group=solve_group),
        grid=(batch // nb, ns),
        in_specs=[pl.BlockSpec((nb, ts, D_MODEL), lambda b, s: (b, s, 0)),
                  _const_spec((1, D_MODEL)),
                  _layer_spec((D_MODEL, GDN_COLS), layer),
                  _layer_spec((D_MODEL, LANES), layer),
                  _const_spec((CONV_WIDTH, GDN_CONV_COLS)),
                  _const_spec((1, LANES)),
                  _const_spec((1, LANES)),
                  _const_spec((1, GDN_DV))],
        out_specs=pl.BlockSpec((nb, ts, GDN_V), lambda b, s: (b, s, 0)),
        out_shape=jax.ShapeDtypeStruct((batch, seq, GDN_V), F32),
        scratch_shapes=[pltpu.VMEM((nb * GDN_HEADS, GDN_DK, GDN_DV), F32),
                        pltpu.VMEM((nb, CARRY_ROWS, GDN_CONV_COLS), F32),
                        pltpu.VMEM((nb, ts, D_MODEL), BF16),
                        pltpu.VMEM((nb, ts, GDN_CONV_COLS), F32),
                        pltpu.VMEM((nb, ts, GDN_V), F32),
                        pltpu.VMEM((nb, ts, LANES), F32),
                        pltpu.VMEM((nb, ts, LANES), F32),
                        pltpu.VMEM((nb * n_chunks, CHUNK, GDN_HEADS * CHUNK), BF16),
                        pltpu.VMEM((n_inst, 2 * CHUNK, GDN_DK), BF16),
                        pltpu.VMEM((nb, ts, GDN_QK), BF16),
                        pltpu.VMEM((n_inst, GDN_DK, CHUNK), BF16),
                        pltpu.VMEM((nb * GDN_HEADS, ts, CHUNK), BF16),
                        pltpu.VMEM((n_inst * SUBLANES, LANES), F32)],
        compiler_params=pltpu.CompilerParams(
            dimension_semantics=("arbitrary", "arbitrary"), vmem_limit_bytes=VMEM_LIMIT),
        name="gdn_mixer",
    )(x3, n_pre, wa, ws, cw, alog_row, dt_row, nw)
    return out.reshape(batch * seq, GDN_V)


PB_Q, PB_K, PB_V, PB_R = 0, GLA_QK, 2 * GLA_QK, 2 * GLA_QK + GLA_V


def _gla_kernel(x_ref, npre_ref, wb_ref, ws_ref, w2_ref, gb_ref, nw_ref, o_ref,
                state_ref, h_ref, pb_ref, b_ref, *, ts):
    n_chunks = ts // CHUNK

    @pl.when(pl.program_id(1) == 0)
    def _():
        state_ref[...] = jnp.zeros_like(state_ref)

    h_ref[...] = _bf(_rms(x_ref[...], npre_ref[...]))

    def project(col, width):
        pb_ref[:, col:col + width] = _dot(h_ref[...], wb_ref[:, col:col + width])

    sm = _dot(h_ref[...], ws_ref[...])
    logit = _dot_x3(sm, w2_ref[...]) + gb_ref[...]
    log_a = -_softplus(-logit) * (1.0 / GLA_TAU)
    causal, _ = _tri_masks()
    tril = _bf(jnp.where(causal, 1.0, 0.0))
    rows = [slice(c * CHUNK, (c + 1) * CHUNK) for c in range(n_chunks)]
    for r in rows:
        b_ref[r, :] = _dot_exact_lhs(tril, log_a[r, :])

    project(PB_Q, 2 * GLA_QK)
    project(PB_V, GLA_DV)
    project(PB_R, GLA_DV)
    nw = nw_ref[...]
    cs = range(n_chunks)
    for h in range(GLA_HEADS):
        kc = slice(h * GLA_DK, (h + 1) * GLA_DK)
        vcol = PB_V + h * GLA_DV
        rcol = PB_R + h * GLA_DV
        b = [b_ref[r, kc] for r in rows]
        q = [pb_ref[r, PB_Q + h * GLA_DK:PB_Q + (h + 1) * GLA_DK] * (GLA_DK ** -0.5) for r in rows]
        k = [pb_ref[r, PB_K + h * GLA_DK:PB_K + (h + 1) * GLA_DK] for r in rows]
        v = [pb_ref[r, vcol:vcol + GLA_DV] for r in rows]
        b_last = [x[CHUNK - 1:CHUNK, :] for x in b]
        b_mid = [x[CHUNK // 2:CHUNK // 2 + 1, :] for x in b]
        attn = [_dot_nt(_bf(q[c] * jnp.exp(b[c] - b_mid[c])), _bf(k[c] * jnp.exp(b_mid[c] - b[c])))
                for c in cs]
        if h + 1 < GLA_HEADS:
            project(vcol + GLA_DV, GLA_DV)
        d_state =[_dot(_bf((k[c] * jnp.exp(b_last[c] - b[c])).T), _bf(v[c])) for c in cs]
        decay_col = [jnp.exp(b[c].T[:, CHUNK - 1:CHUNK]) for c in cs]
        o_intra = [_dot(_bf(jnp.where(causal, attn[c], 0.0)), _bf(v[c])) for c in cs]
        if h + 1 < GLA_HEADS:
            project(rcol + GLA_DV, GLA_DV)
        state = state_ref[h]
        o_inter = []
        for c in cs:
            o_inter.append(_dot(_bf(q[c] * jnp.exp(b[c])), _bf(state)))
            state = state * decay_col[c] + d_state[c]
        state_ref[h] = state
        for c in cs:
            o_ref[rows[c], h * GLA_DV:(h + 1) * GLA_DV] = (
                _rms(o_intra[c] + o_inter[c], nw) * _silu(pb_ref[rows[c], rcol:rcol + GLA_DV]))


def _gla(x2, n_pre, wb, ws, w2pad, gate_b, nw, layer, batch, seq, ts=512):
    ns = seq // ts
    row = lambda b, s: (b * ns + s, 0)
    return pl.pallas_call(
        functools.partial(_gla_kernel, ts=ts),
        grid=(batch, ns),
        in_specs=[pl.BlockSpec((ts, D_MODEL), row),
                  _const_spec((1, D_MODEL)),
                  _layer_spec((D_MODEL, GLA_COLS), layer),
                  _layer_spec((D_MODEL, LANES), layer),
                  _const_spec((LANES, GLA_QK)),
                  _const_spec((1, GLA_QK)),
                  _const_spec((1, GLA_DV))],
        out_specs=pl.BlockSpec((ts, GLA_V), row),
        out_shape=jax.ShapeDtypeStruct((batch * seq, GLA_V), F32),
        scratch_shapes=[pltpu.VMEM((GLA_HEADS, GLA_DK, GLA_DV), F32),
                        pltpu.VMEM((ts, D_MODEL), BF16),
                        pltpu.VMEM((ts, GLA_COLS), F32),
                        pltpu.VMEM((ts, GLA_QK), F32)],
        compiler_params=pltpu.CompilerParams(
            dimension_semantics=("arbitrary", "arbitrary"), vmem_limit_bytes=VMEM_LIMIT),
        name="gla_mixer",
    )(x2, n_pre, wb, ws, w2pad, gate_b, nw)


def _merge_mlp_kernel(ya_ref, yb_ref, x_ref, n_mix_ref, wg_ref, woa_ref, wob_ref, wo_ref,
                      n_post_ref, n_pre_ref, wup_ref, wdn_ref, n_mlp_ref, o_ref):
    x = x_ref[...]
    hg = _bf(_rms(x, n_mix_ref[...]))
    ya = _dot(_bf(ya_ref[...]), woa_ref[...])
    merged = _sigmoid(_dot(hg, wg_ref[:, :D_MODEL])) * ya
    yb = _dot(_bf(yb_ref[...]), wob_ref[...])
    merged = merged + _sigmoid(_dot(hg, wg_ref[:, D_MODEL:])) * yb
    y = _dot(_bf(merged), wo_ref[...])
    x1 = x + _rms(y, n_post_ref[...])
    h = _bf(_rms(x1, n_pre_ref[...]))
    acc = jnp.zeros(x1.shape, F32)
    for c in range(D_FF // D_MODEL):
        cs = slice(c * D_MODEL, (c + 1) * D_MODEL)
        u = jnp.square(jnp.maximum(_dot(h, wup_ref[:, cs]), 0.0))
        acc = acc + _dot(_bf(u), wdn_ref[cs, :])
    o_ref[...] = x1 + _rms(acc, n_mlp_ref[...])


def _merge_mlp(ya, yb, x2, n_mix, wg, woa, wob, wo, n_post, n_pre, wup, wdn, n_mlp, layer, tm=512):
    t = x2.shape[0]
    row = lambda i: (i, 0)
    return pl.pallas_call(
        _merge_mlp_kernel,
        grid=(t // tm,),
        in_specs=[pl.BlockSpec((tm, GDN_V), row),
                  pl.BlockSpec((tm, GLA_V), row),
                  pl.BlockSpec((tm, D_MODEL), row),
                  _const_spec((1, D_MODEL)),
                  _layer_spec((D_MODEL, GATE_COLS), layer),
                  _layer_spec((GDN_V, D_MODEL), layer),
                  _layer_spec((GLA_V, D_MODEL), layer),
                  _layer_spec((D_MODEL, D_MODEL), layer),
                  _const_spec((1, D_MODEL)),
                  _const_spec((1, D_MODEL)),
                  _layer_spec((D_MODEL, D_FF), layer),
                  _layer_spec((D_FF, D_MODEL), layer),
                  _const_spec((1, D_MODEL))],
        out_specs=pl.BlockSpec((tm, D_MODEL), row),
        out_shape=jax.ShapeDtypeStruct((t, D_MODEL), F32),
        compiler_params=pltpu.CompilerParams(
            dimension_semantics=("arbitrary",), vmem_limit_bytes=VMEM_LIMIT),
        name="merge_mlp",
    )(ya, yb, x2, n_mix, wg, woa, wob, wo, n_post, n_pre, wup, wdn, n_mlp)


def _pad_row(v, offset):
    return jnp.zeros((1, LANES), F32).at[0, offset:offset + v.shape[0]].set(v)


def kernel(x, w_in, conv_w, a_log, dt_bias, gdn_norm, gla_gate_w2, gla_gate_b, gla_norm,
           w_out_a, w_out_b, w_o, norm_mix_pre, norm_mix_post, norm_mlp_pre, norm_mlp_post,
           w_mlp_up, w_mlp_down):
    batch, seq, d = x.shape
    x2 = x.reshape(batch * seq, d)
    w_a = w_in[:, :, OFF_GDN:OFF_GDN + GDN_COLS].astype(BF16)
    w_b = w_in[:, :, OFF_GLA:OFF_GLA + GLA_COLS].astype(BF16)
    w_g = w_in[:, :, OFF_GATE:OFF_GATE + GATE_COLS].astype(BF16)
    n_small = 2 * GDN_HEADS + GLA_GATE_RANK
    w_s = jnp.concatenate(
        [w_in[:, :, OFF_BETA:OFF_BETA + 2 * GDN_HEADS], w_in[:, :, OFF_GLR:OFF_GLR + GLA_GATE_RANK],
         jnp.zeros((DEPTH, d, LANES - n_small), w_in.dtype)], axis=-1).astype(BF16)
    woa_b, wob_b, wo_b = w_out_a.astype(BF16), w_out_b.astype(BF16), w_o.astype(BF16)
    wup_b, wdn_b = w_mlp_up.astype(BF16), w_mlp_down.astype(BF16)
    w2pad = jnp.zeros((DEPTH, LANES, GLA_QK), F32).at[:, SM_GLR:SM_GLR + GLA_GATE_RANK].set(
        gla_gate_w2)
    for l in range(DEPTH):
        n_mix = norm_mix_pre[l][None, :]
        ya = _gdn(x2, n_mix, w_a, w_s, conv_w[l], _pad_row(a_log[l], SM_DECAY),
                  _pad_row(dt_bias[l], SM_DECAY), gdn_norm[l][None, :], l, batch, seq)
        yb = _gla(x2, n_mix, w_b, w_s, w2pad[l], gla_gate_b[l][None, :], gla_norm[l][None, :],
                  l, batch, seq)
        x2 = _merge_mlp(ya, yb, x2, n_mix, w_g, woa_b, wob_b, wo_b,
                        norm_mix_post[l][None, :], norm_mlp_pre[l][None, :],
                        wup_b, wdn_b, norm_mlp_post[l][None, :], l)
    return x2.reshape(batch, seq, d)
```

```python
import functools

import jax
import jax.numpy as jnp
from jax import lax
from jax.experimental import pallas as pl
from jax.experimental.pallas import tpu as pltpu

D_MODEL = 1024
DEPTH = 4
GDN_HEADS = 4
GDN_DK = 128
GDN_DV = 128
CONV_WIDTH = 4
GLA_HEADS = 4
GLA_DK = 128
GLA_DV = 256
GLA_GATE_RANK = 16
GLA_TAU = 16.0
CHUNK = 64
D_FF = 4 * D_MODEL
EPS = 1e-6

GDN_QK = GDN_HEADS * GDN_DK
GDN_V = GDN_HEADS * GDN_DV
GLA_QK = GLA_HEADS * GLA_DK
GLA_V = GLA_HEADS * GLA_DV

LANES = 128
SUBLANES = 8
CARRY_ROWS = SUBLANES

GDN_COLS = 2 * GDN_QK + 2 * GDN_V
GLA_COLS = 2 * GLA_QK + 2 * GLA_V
GATE_COLS = 2 * D_MODEL
OFF_GDN = 0
OFF_BETA = OFF_GDN + GDN_COLS
OFF_GLA = OFF_BETA + 2 * GDN_HEADS
OFF_GLR = OFF_GLA + GLA_COLS
OFF_GATE = OFF_GLR + GLA_GATE_RANK
GDN_CONV_COLS = 2 * GDN_QK + GDN_V
SM_BETA = 0
SM_DECAY = GDN_HEADS
SM_GLR = 2 * GDN_HEADS

VMEM_LIMIT = 56 * 1024 * 1024

F32 = jnp.float32
BF16 = jnp.bfloat16


def _dot(a, b):
    return jnp.dot(a, b, preferred_element_type=F32)


def _dot_nt(a, b):
    return lax.dot_general(a, b, (((1,), (1,)), ((), ())), preferred_element_type=F32)


def _bf(x):
    return x.astype(BF16)


def _dot_exact_lhs(a_bf, x):
    n = x.shape[1]
    hi = _bf(x)
    lo = _bf(x - hi.astype(F32))
    y = _dot(a_bf, jnp.concatenate([hi, lo], axis=1))
    return y[:, :n] + y[:, n:]


def _dot_x3(a, b):
    a_hi = _bf(a)
    a_lo = _bf(a - a_hi.astype(F32))
    b_hi = _bf(b)
    b_lo = _bf(b - b_hi.astype(F32))
    return (_dot(a_hi, b_lo) + _dot(a_lo, b_hi)) + _dot(a_hi, b_hi)


def _rms(x, w):
    return x * lax.rsqrt(jnp.mean(x * x, axis=-1, keepdims=True) + EPS) * w


def _sigmoid(x):
    return 1.0 / (1.0 + jnp.exp(-x))


def _silu(x):
    return x * _sigmoid(x)


def _softplus(x):
    return jnp.maximum(x, 0.0) + jnp.log(1.0 + jnp.exp(-jnp.abs(x)))


def _const_spec(shape):
    nd = len(shape)
    return pl.BlockSpec(shape, lambda *_: (0,) * nd, pipeline_mode=pl.Buffered(1))


def _layer_spec(shape, layer):
    nd = len(shape)
    return pl.BlockSpec((None,) + tuple(shape), lambda *_: (layer,) + (0,) * nd,
                        pipeline_mode=pl.Buffered(1))


def _tri_masks():
    row = lax.broadcasted_iota(jnp.int32, (CHUNK, CHUNK), 0)
    col = lax.broadcasted_iota(jnp.int32, (CHUNK, CHUNK), 1)
    return row >= col, row > col


def _gdn_kernel(x_ref, npre_ref, wa32_ref, ws_ref, cw_ref, alog_ref, dt_ref, nw_ref, o_ref,
                wa_ref, state_ref, carry_ref, h_ref, act_ref, z_ref, beta_ref, g_ref,
                tinv_ref, kg_ref, qd_ref, kdt_ref, qk_ref, egl_ref, *, nb, ts, solve_group):
    n_chunks = ts // CHUNK

    @pl.when((pl.program_id(0) == 0) & (pl.program_id(1) == 0))
    def _():
        wa_ref[...] = _bf(wa32_ref[...])

    @pl.when(pl.program_id(1) == 0)
    def _():
        state_ref[...] = jnp.zeros_like(state_ref)
        carry_ref[...] = jnp.zeros_like(carry_ref)

    tile = 2 * LANES
    for bb in range(nb):
        h_ref[bb] = _bf(_rms(x_ref[bb], npre_ref[...]))
    for bb in range(nb):
        sm = _dot(h_ref[bb], ws_ref[...])
        beta_ref[bb] = _sigmoid(sm)
        g_ref[bb] = -(jnp.exp(alog_ref[...]) * _softplus(sm + dt_ref[...]))
        for t in range(GDN_COLS // tile):
            proj = _dot(h_ref[bb], wa_ref[:, t * tile:(t + 1) * tile])
            if t * tile >= GDN_CONV_COLS:
                z_ref[bb, :, t * tile - GDN_CONV_COLS:(t + 1) * tile - GDN_CONV_COLS] = proj
                continue
            for j in range(t * tile // LANES, (t + 1) * tile // LANES):
                cs = slice(j * LANES, (j + 1) * LANES)
                x = proj[:, j * LANES - t * tile:(j + 1) * LANES - t * tile]
                xx = jnp.concatenate([carry_ref[bb, :, cs], x], axis=0)
                base = CARRY_ROWS - (CONV_WIDTH - 1)
                acc = xx[base:base + ts] * cw_ref[0:1, cs]
                for k in range(1, CONV_WIDTH):
                    acc = acc + xx[base + k:base + k + ts] * cw_ref[k:k + 1, cs]
                carry_ref[bb, :, cs] = x[ts - CARRY_ROWS:ts]
                a = _silu(acc)
                if j < 2 * GDN_HEADS:
                    a = a * lax.rsqrt(jnp.sum(a * a, axis=-1, keepdims=True) + EPS)
                    if j < GDN_HEADS:
                        a = a * (GDN_DK ** -0.5)
                act_ref[bb, :, cs] = a

    causal, strict = _tri_masks()
    tril = _bf(jnp.where(causal, 1.0, 0.0))
    eye = jnp.where(causal & jnp.logical_not(strict), 1.0, 0.0).astype(F32)
    rrow = lax.broadcasted_iota(jnp.int32, (CHUNK, 4 * CHUNK), 0)
    rcol = lax.broadcasted_iota(jnp.int32, (CHUNK, 4 * CHUNK), 1)
    rhs_mask = ((rcol < CHUNK) & (rrow > rcol)) | (rcol >= 2 * CHUNK)
    hv = [slice(h * GDN_DV, (h + 1) * GDN_DV) for h in range(GDN_HEADS)]
    packed = GDN_HEADS * CHUNK
    prow = lax.broadcasted_iota(jnp.int32, (packed, packed), 0)
    pcol = lax.broadcasted_iota(jnp.int32, (packed, packed), 1)
    blk_mask = (prow // CHUNK) == (pcol // CHUNK)
    eye_p = jnp.concatenate([eye] * GDN_HEADS, axis=1)

    def solve_body(grp, carry):
        insts = [(bb, ci, h) for bb in range(nb) for ci in range(solve_group)
                 for h in range(GDN_HEADS)]
        n = len(insts)
        chunk_ids = [grp * solve_group + ci for ci in range(solve_group)]
        rows = [pl.ds(pl.multiple_of(c * CHUNK, CHUNK), CHUNK) for c in chunk_ids]
        beta_c = {(bb, ci): beta_ref[bb, rows[ci], :] for bb in range(nb)
                  for ci in range(solve_group)}
        g_c = {(bb, ci): g_ref[bb, rows[ci], :] for bb in range(nb) for ci in range(solve_group)}

        dg = [_dot_exact_lhs(tril, jnp.where(
            rhs_mask, g_c[bb, ci][:, SM_DECAY + h:SM_DECAY + h + 1], 0.0)) for bb, ci, h in insts]
        q = [act_ref[bb, rows[ci], h * GDN_DK:(h + 1) * GDN_DK] for bb, ci, h in insts]
        k = [act_ref[bb, rows[ci], GDN_QK + h * GDN_DK:GDN_QK + (h + 1) * GDN_DK]
             for bb, ci, h in insts]
        kb = [k[i] * beta_c[bb, ci][:, SM_BETA + h:SM_BETA + h + 1]
              for i, (bb, ci, h) in enumerate(insts)]
        kq = [_dot_nt(_bf(jnp.concatenate([kb[i], q[i]], axis=0)), _bf(k[i])) for i in range(n)]
        decay = [jnp.where(causal, jnp.exp(d[:, :CHUNK]), 0.0) for d in dg]
        gc = [d[:, 2 * CHUNK:] for d in dg]
        egc = [jnp.exp(x) for x in gc]
        a_mat = [jnp.where(strict, kq[i][:CHUNK] * decay[i], 0.0) for i in range(n)]
        groups = [(bb, ci) for bb in range(nb) for ci in range(solve_group)]
        ng = range(len(groups))
        a_p = [jnp.concatenate([a_mat[gi * GDN_HEADS + h] for h in range(GDN_HEADS)], axis=1)
               for gi in ng]

        def block_diag(x_bf):
            return jnp.where(blk_mask, jnp.concatenate([x_bf] * GDN_HEADS, axis=0),
                             jnp.zeros((), BF16))

        p = [_bf(x) for x in a_p]
        pd = [block_diag(x) for x in p]
        tinv = [eye_p - x for x in a_p]
        power = 2
        while True:
            pp = [_dot(p[gi], pd[gi]) for gi in ng]
            p = [_bf(x) for x in pp]
            pd = [block_diag(x) for x in p]
            xp = [_dot(_bf(tinv[gi]), pd[gi]) for gi in ng]
            tinv = [tinv[gi] + xp[gi] for gi in ng]
            power *= 2
            if power >= CHUNK:
                break
        for gi, (bb, ci) in enumerate(groups):
            tinv_ref[bb * n_chunks + chunk_ids[ci]] = _bf(tinv[gi])
        for i, (bb, ci, h) in enumerate(insts):
            idx = (bb * n_chunks + chunk_ids[ci]) * GDN_HEADS + h
            gl = gc[i][CHUNK - 1:CHUNK, :]
            kg = k[i] * egc[i]
            kg_hi = _bf(kg)
            kg_ref[idx, :CHUNK, :] = kg_hi
            kg_ref[idx, CHUNK:, :] = _bf(kg - kg_hi.astype(F32))
            qd_ref[bb, rows[ci], hv[h]] = _bf(q[i] * egc[i])
            kdt_ref[idx] = _bf((k[i] * jnp.exp(gl - gc[i])).T)
            qk_ref[bb * GDN_HEADS + h, rows[ci], :] = _bf(kq[i][CHUNK:] * decay[i])
            egl_ref[pl.ds(pl.multiple_of(idx * SUBLANES, SUBLANES), SUBLANES), :] = (
                jnp.broadcast_to(jnp.exp(gl), (SUBLANES, LANES)))
        return carry

    lax.fori_loop(0, n_chunks // solve_group, solve_body, 0)

    nw = nw_ref[...]
    chains = [(bb, h) for bb in range(nb) for h in range(GDN_HEADS)]
    nc = range(len(chains))
    states = [state_ref[bb * GDN_HEADS + h] for bb, h in chains]
    for c in range(n_chunks):
        rows = slice(c * CHUNK, (c + 1) * CHUNK)
        idx = [(bb * n_chunks + c) * GDN_HEADS + h for bb, h in chains]
        s_hi = [_bf(states[j]) for j in nc]
        s_lo = [_bf(states[j] - s_hi[j].astype(F32)) for j in nc]
        ks_a = [_dot(kg_ref[idx[j]], s_hi[j]) for j in nc]
        ks_b = [_dot(kg_ref[idx[j], :CHUNK, :], s_lo[j]) for j in nc]
        resid = [(act_ref[bb, rows, 2 * GDN_QK + h * GDN_DV:2 * GDN_QK + (h + 1) * GDN_DV]
                  - ((ks_b[j] + ks_a[j][CHUNK:]) + ks_a[j][:CHUNK]))
                 * beta_ref[bb, rows, SM_BETA + h:SM_BETA + h + 1]
                 for j, (bb, h) in enumerate(chains)]
        zero = jnp.zeros((CHUNK, GDN_DV), BF16)

        def block_diag_heads(xs):
            return jnp.concatenate(
                [jnp.concatenate([xs[h] if hh == h else zero for hh in range(GDN_HEADS)], axis=1)
                 for h in range(GDN_HEADS)], axis=0)

        v_new = []
        for bb in range(nb):
            r_bf = [_bf(resid[bb * GDN_HEADS + h]) for h in range(GDN_HEADS)]
            v_all = _dot(tinv_ref[bb * n_chunks + c], block_diag_heads(r_bf))
            v_new += [_bf(v_all[:, hv[h]]) for h in range(GDN_HEADS)]
        ds = [_dot(kdt_ref[idx[j]], v_new[j]) for j in nc]
        states = [states[j] * egl_ref[idx[j] * SUBLANES:idx[j] * SUBLANES + 1, :] + ds[j]
                  for j in nc]
        o = [_dot(qd_ref[bb, rows, hv[h]], s_hi[j])
             + _dot(qk_ref[bb * GDN_HEADS + h, rows, :], v_new[j])
             for j, (bb, h) in enumerate(chains)]
        for j, (bb, h) in enumerate(chains):
            o_ref[bb, rows, hv[h]] = _rms(o[j], nw) * _silu(z_ref[bb, rows, hv[h]])
    for j, (bb, h) in enumerate(chains):
        state_ref[bb * GDN_HEADS + h] = states[j]


def _gdn(x2, n_pre, wa, ws, cw, alog_row, dt_row, nw, layer, batch, seq,
         nb=2, ts=512, solve_group=8):
    ns = seq // ts
    n_chunks = ts // CHUNK
    n_inst = nb * n_chunks * GDN_HEADS
    x3 = x2.reshape(batch, seq, D_MODEL)
    out = pl.pallas_call(
        functools.partial(_gdn_kernel, nb=nb, ts=ts, solve_group=solve_group),
        grid=(batch // nb, ns),
        in_specs=[pl.BlockSpec((nb, ts, D_MODEL), lambda b, s: (b, s, 0)),
                  _const_spec((1, D_MODEL)),
                  pl.BlockSpec((None, D_MODEL, GDN_COLS), lambda *_: (layer, 0, OFF_GDN // GDN_COLS),
                               pipeline_mode=pl.Buffered(1)),
                  _layer_spec((D_MODEL, LANES), layer),
                  _const_spec((CONV_WIDTH, GDN_CONV_COLS)),
                  _const_spec((1, LANES)),
                  _const_spec((1, LANES)),
                  _const_spec((1, GDN_DV))],
        out_specs=pl.BlockSpec((nb, ts, GDN_V), lambda b, s: (b, s, 0)),
        out_shape=jax.ShapeDtypeStruct((batch, seq, GDN_V), F32),
        scratch_shapes=[pltpu.VMEM((D_MODEL, GDN_COLS), BF16),
                        pltpu.VMEM((nb * GDN_HEADS, GDN_DK, GDN_DV), F32),
                        pltpu.VMEM((nb, CARRY_ROWS, GDN_CONV_COLS), F32),
                        pltpu.VMEM((nb, ts, D_MODEL), BF16),
                        pltpu.VMEM((nb, ts, GDN_CONV_COLS), F32),
                        pltpu.VMEM((nb, ts, GDN_V), F32),
                        pltpu.VMEM((nb, ts, LANES), F32),
                        pltpu.VMEM((nb, ts, LANES), F32),
                        pltpu.VMEM((nb * n_chunks, CHUNK, GDN_HEADS * CHUNK), BF16),
                        pltpu.VMEM((n_inst, 2 * CHUNK, GDN_DK), BF16),
                        pltpu.VMEM((nb, ts, GDN_QK), BF16),
                        pltpu.VMEM((n_inst, GDN_DK, CHUNK), BF16),
                        pltpu.VMEM((nb * GDN_HEADS, ts, CHUNK), BF16),
                        pltpu.VMEM((n_inst * SUBLANES, LANES), F32)],
        compiler_params=pltpu.CompilerParams(
            dimension_semantics=("arbitrary", "arbitrary"), vmem_limit_bytes=VMEM_LIMIT),
        name="gdn_mixer",
    )(x3, n_pre, wa, ws, cw, alog_row, dt_row, nw)
    return out.reshape(batch * seq, GDN_V)


PB_Q, PB_K, PB_V, PB_R = 0, GLA_QK, 2 * GLA_QK, 2 * GLA_QK + GLA_V


def _gla_kernel(x_ref, npre_ref, wb32_ref, ws_ref, w2_ref, gb_ref, nw_ref, o_ref,
                wb_ref, state_ref, h_ref, pb_ref, b_ref, *, ts):
    n_chunks = ts // CHUNK

    @pl.when((pl.program_id(0) == 0) & (pl.program_id(1) == 0))
    def _():
        wb_ref[...] = _bf(wb32_ref[...])

    @pl.when(pl.program_id(1) == 0)
    def _():
        state_ref[...] = jnp.zeros_like(state_ref)

    h_ref[...] = _bf(_rms(x_ref[...], npre_ref[...]))

    def project(col, width):
        pb_ref[:, col:col + width] = _dot(h_ref[...], wb_ref[:, col:col + width])

    sm = _dot(h_ref[...], ws_ref[...])
    logit = _dot_x3(sm, w2_ref[...]) + gb_ref[...]
    log_a = -_softplus(-logit) * (1.0 / GLA_TAU)
    causal, _ = _tri_masks()
    tril = _bf(jnp.where(causal, 1.0, 0.0))
    rows = [slice(c * CHUNK, (c + 1) * CHUNK) for c in range(n_chunks)]
    for r in rows:
        b_ref[r, :] = _dot_exact_lhs(tril, log_a[r, :])

    project(PB_Q, 2 * GLA_QK)
    project(PB_V, GLA_DV)
    project(PB_R, GLA_DV)
    nw = nw_ref[...]
    cs = range(n_chunks)
    for h in range(GLA_HEADS):
        kc = slice(h * GLA_DK, (h + 1) * GLA_DK)
        vcol = PB_V + h * GLA_DV
        rcol = PB_R + h * GLA_DV
        b = [b_ref[r, kc] for r in rows]
        q = [pb_ref[r, PB_Q + h * GLA_DK:PB_Q + (h + 1) * GLA_DK] * (GLA_DK ** -0.5) for r in rows]
        k = [pb_ref[r, PB_K + h * GLA_DK:PB_K + (h + 1) * GLA_DK] for r in rows]
        v = [pb_ref[r, vcol:vcol + GLA_DV] for r in rows]
        b_last = [x[CHUNK - 1:CHUNK, :] for x in b]
        b_mid = [x[CHUNK // 2:CHUNK // 2 + 1, :] for x in b]
        attn = [_dot_nt(_bf(q[c] * jnp.exp(b[c] - b_mid[c])), _bf(k[c] * jnp.exp(b_mid[c] - b[c])))
                for c in cs]
        if h + 1 < GLA_HEADS:
            project(vcol + GLA_DV, GLA_DV)
        d_state =[_dot(_bf((k[c] * jnp.exp(b_last[c] - b[c])).T), _bf(v[c])) for c in cs]
        decay_col = [jnp.exp(b[c].T[:, CHUNK - 1:CHUNK]) for c in cs]
        o_intra = [_dot(_bf(jnp.where(causal, attn[c], 0.0)), _bf(v[c])) for c in cs]
        if h + 1 < GLA_HEADS:
            project(rcol + GLA_DV, GLA_DV)
        state = state_ref[h]
        o_inter = []
        for c in cs:
            o_inter.append(_dot(_bf(q[c] * jnp.exp(b[c])), _bf(state)))
            state = state * decay_col[c] + d_state[c]
        state_ref[h] = state
        for c in cs:
            o_ref[rows[c], h * GLA_DV:(h + 1) * GLA_DV] = (
                _rms(o_intra[c] + o_inter[c], nw) * _silu(pb_ref[rows[c], rcol:rcol + GLA_DV]))


def _gla(x2, n_pre, wb, ws, w2pad, gate_b, nw, layer, batch, seq, ts=512):
    ns = seq // ts
    row = lambda b, s: (b * ns + s, 0)
    return pl.pallas_call(
        functools.partial(_gla_kernel, ts=ts),
        grid=(batch, ns),
        in_specs=[pl.BlockSpec((ts, D_MODEL), row),
                  _const_spec((1, D_MODEL)),
                  _layer_spec((D_MODEL, GLA_COLS), layer),
                  _layer_spec((D_MODEL, LANES), layer),
                  _const_spec((LANES, GLA_QK)),
                  _const_spec((1, GLA_QK)),
                  _const_spec((1, GLA_DV))],
        out_specs=pl.BlockSpec((ts, GLA_V), row),
        out_shape=jax.ShapeDtypeStruct((batch * seq, GLA_V), F32),
        scratch_shapes=[pltpu.VMEM((D_MODEL, GLA_COLS), BF16),
                        pltpu.VMEM((GLA_HEADS, GLA_DK, GLA_DV), F32),
                        pltpu.VMEM((ts, D_MODEL), BF16),
                        pltpu.VMEM((ts, GLA_COLS), F32),
                        pltpu.VMEM((ts, GLA_QK), F32)],
        compiler_params=pltpu.CompilerParams(
            dimension_semantics=("arbitrary", "arbitrary"), vmem_limit_bytes=VMEM_LIMIT),
        name="gla_mixer",
    )(x2, n_pre, wb, ws, w2pad, gate_b, nw)


def _merge_mlp_kernel(ya_ref, yb_ref, x_ref, n_mix_ref, wg_ref, woa_ref, wob_ref, wo_ref,
                      n_post_ref, n_pre_ref, wup_ref, wdn_ref, n_mlp_ref, o_ref):
    x = x_ref[...]
    hg = _bf(_rms(x, n_mix_ref[...]))
    ya = _dot(_bf(ya_ref[...]), woa_ref[...])
    merged = _sigmoid(_dot(hg, wg_ref[:, :D_MODEL])) * ya
    yb = _dot(_bf(yb_ref[...]), wob_ref[...])
    merged = merged + _sigmoid(_dot(hg, wg_ref[:, D_MODEL:])) * yb
    y = _dot(_bf(merged), wo_ref[...])
    x1 = x + _rms(y, n_post_ref[...])
    h = _bf(_rms(x1, n_pre_ref[...]))
    acc = jnp.zeros(x1.shape, F32)
    for c in range(D_FF // D_MODEL):
        cs = slice(c * D_MODEL, (c + 1) * D_MODEL)
        u = jnp.square(jnp.maximum(_dot(h, wup_ref[:, cs]), 0.0))
        acc = acc + _dot(_bf(u), wdn_ref[cs, :])
    o_ref[...] = x1 + _rms(acc, n_mlp_ref[...])


def _merge_mlp(ya, yb, x2, n_mix, wg, woa, wob, wo, n_post, n_pre, wup, wdn, n_mlp, layer, tm=512):
    t = x2.shape[0]
    row = lambda i: (i, 0)
    return pl.pallas_call(
        _merge_mlp_kernel,
        grid=(t // tm,),
        in_specs=[pl.BlockSpec((tm, GDN_V), row),
                  pl.BlockSpec((tm, GLA_V), row),
                  pl.BlockSpec((tm, D_MODEL), row),
                  _const_spec((1, D_MODEL)),
                  _layer_spec((D_MODEL, GATE_COLS), layer),
                  _layer_spec((GDN_V, D_MODEL), layer),
                  _layer_spec((GLA_V, D_MODEL), layer),
                  _layer_spec((D_MODEL, D_MODEL), layer),
                  _const_spec((1, D_MODEL)),
                  _const_spec((1, D_MODEL)),
                  _layer_spec((D_MODEL, D_FF), layer),
                  _layer_spec((D_FF, D_MODEL), layer),
                  _const_spec((1, D_MODEL))],
        out_specs=pl.BlockSpec((tm, D_MODEL), row),
        out_shape=jax.ShapeDtypeStruct((t, D_MODEL), F32),
        compiler_params=pltpu.CompilerParams(
            dimension_semantics=("arbitrary",), vmem_limit_bytes=VMEM_LIMIT),
        name="merge_mlp",
    )(ya, yb, x2, n_mix, wg, woa, wob, wo, n_post, n_pre, wup, wdn, n_mlp)


def _pad_row(v, offset):
    return jnp.zeros((1, LANES), F32).at[0, offset:offset + v.shape[0]].set(v)


def kernel(x, w_in, conv_w, a_log, dt_bias, gdn_norm, gla_gate_w2, gla_gate_b, gla_norm,
           w_out_a, w_out_b, w_o, norm_mix_pre, norm_mix_post, norm_mlp_pre, norm_mlp_post,
           w_mlp_up, w_mlp_down):
    batch, seq, d = x.shape
    x2 = x.reshape(batch * seq, d)
    w_a = w_in
    w_b = w_in[:, :, OFF_GLA:OFF_GLA + GLA_COLS]
    w_g = w_in[:, :, OFF_GATE:OFF_GATE + GATE_COLS].astype(BF16)
    n_small = 2 * GDN_HEADS + GLA_GATE_RANK
    w_s = jnp.concatenate(
        [w_in[:, :, OFF_BETA:OFF_BETA + 2 * GDN_HEADS], w_in[:, :, OFF_GLR:OFF_GLR + GLA_GATE_RANK],
         jnp.zeros((DEPTH, d, LANES - n_small), w_in.dtype)], axis=-1).astype(BF16)
    woa_b, wob_b, wo_b = w_out_a.astype(BF16), w_out_b.astype(BF16), w_o.astype(BF16)
    wup_b, wdn_b = w_mlp_up.astype(BF16), w_mlp_down.astype(BF16)
    w2pad = jnp.zeros((DEPTH, LANES, GLA_QK), F32).at[:, SM_GLR:SM_GLR + GLA_GATE_RANK].set(
        gla_gate_w2)
    for l in range(DEPTH):
        n_mix = norm_mix_pre[l][None, :]
        ya = _gdn(x2, n_mix, w_a, w_s, conv_w[l], _pad_row(a_log[l], SM_DECAY),
                  _pad_row(dt_bias[l], SM_DECAY), gdn_norm[l][None, :], l, batch, seq)
        yb = _gla(x2, n_mix, w_b, w_s, w2pad[l], gla_gate_b[l][None, :], gla_norm[l][None, :],
                  l, batch, seq)
        x2 = _merge_mlp(ya, yb, x2, n_mix, w_g, woa_b, wob_b, wo_b,
                        norm_mix_post[l][None, :], norm_mlp_pre[l][None, :],
                        wup_b, wdn_b, norm_mlp_post[l][None, :], l)
    return x2.reshape(batch, seq, d)
```

```python
import functools

import jax
import jax.numpy as jnp
from jax import lax
from jax.experimental import pallas as pl
from jax.experimental.pallas import tpu as pltpu

D_MODEL = 1024
DEPTH = 4
GDN_HEADS = 4
GDN_DK = 128
GDN_DV = 128
CONV_WIDTH = 4
GLA_HEADS = 4
GLA_DK = 128
GLA_DV = 256
GLA_GATE_RANK = 16
GLA_TAU = 16.0
CHUNK = 64
D_FF = 4 * D_MODEL
EPS = 1e-6

GDN_QK = GDN_HEADS * GDN_DK
GDN_V = GDN_HEADS * GDN_DV
GLA_QK = GLA_HEADS * GLA_DK
GLA_V = GLA_HEADS * GLA_DV

LANES = 128
SUBLANES = 8
CARRY_ROWS = SUBLANES

GDN_COLS = 2 * GDN_QK + 2 * GDN_V
GLA_COLS = 2 * GLA_QK + 2 * GLA_V
GATE_COLS = 2 * D_MODEL
OFF_GDN = 0
OFF_BETA = OFF_GDN + GDN_COLS
OFF_GLA = OFF_BETA + 2 * GDN_HEADS
OFF_GLR = OFF_GLA + GLA_COLS
OFF_GATE = OFF_GLR + GLA_GATE_RANK
GDN_CONV_COLS = 2 * GDN_QK + GDN_V
SM_BETA = 0
SM_DECAY = GDN_HEADS
SM_GLR = 2 * GDN_HEADS

VMEM_LIMIT = 56 * 1024 * 1024

F32 = jnp.float32
BF16 = jnp.bfloat16


def _dot(a, b):
    return jnp.dot(a, b, preferred_element_type=F32)


def _dot_nt(a, b):
    return lax.dot_general(a, b, (((1,), (1,)), ((), ())), preferred_element_type=F32)


def _bf(x):
    return x.astype(BF16)


def _dot_exact_lhs(a_bf, x):
    n = x.shape[1]
    hi = _bf(x)
    lo = _bf(x - hi.astype(F32))
    y = _dot(a_bf, jnp.concatenate([hi, lo], axis=1))
    return y[:, :n] + y[:, n:]


def _dot_x3(a, b):
    a_hi = _bf(a)
    a_lo = _bf(a - a_hi.astype(F32))
    b_hi = _bf(b)
    b_lo = _bf(b - b_hi.astype(F32))
    return (_dot(a_hi, b_lo) + _dot(a_lo, b_hi)) + _dot(a_hi, b_hi)


def _rms(x, w):
    return x * lax.rsqrt(jnp.mean(x * x, axis=-1, keepdims=True) + EPS) * w


def _sigmoid(x):
    return 1.0 / (1.0 + jnp.exp(-x))


def _silu(x):
    return x * _sigmoid(x)


def _softplus(x):
    return jnp.maximum(x, 0.0) + jnp.log(1.0 + jnp.exp(-jnp.abs(x)))


def _const_spec(shape):
    nd = len(shape)
    return pl.BlockSpec(shape, lambda *_: (0,) * nd, pipeline_mode=pl.Buffered(1))


def _layer_spec(shape, layer):
    nd = len(shape)
    return pl.BlockSpec((None,) + tuple(shape), lambda *_: (layer,) + (0,) * nd,
                        pipeline_mode=pl.Buffered(1))


def _tri_masks():
    row = lax.broadcasted_iota(jnp.int32, (CHUNK, CHUNK), 0)
    col = lax.broadcasted_iota(jnp.int32, (CHUNK, CHUNK), 1)
    return row >= col, row > col


def _gdn_kernel(x_ref, npre_ref, wa_ref, ws_ref, cw_ref, alog_ref, dt_ref, nw_ref, o_ref,
                state_ref, carry_ref, h_ref, act_ref, z_ref, beta_ref, g_ref,
                tinv_ref, kg_ref, qd_ref, kdt_ref, qk_ref, egl_ref, *, nb, ts, solve_group):
    n_chunks = ts // CHUNK

    @pl.when(pl.program_id(1) == 0)
    def _():
        state_ref[...] = jnp.zeros_like(state_ref)
        carry_ref[...] = jnp.zeros_like(carry_ref)

    tile = 2 * LANES
    for bb in range(nb):
        h_ref[bb] = _bf(_rms(x_ref[bb], npre_ref[...]))
    for bb in range(nb):
        sm = _dot(h_ref[bb], ws_ref[...])
        beta_ref[bb] = _sigmoid(sm)
        g_ref[bb] = -(jnp.exp(alog_ref[...]) * _softplus(sm + dt_ref[...]))
        for t in range(GDN_COLS // tile):
            proj = _dot(h_ref[bb], wa_ref[:, t * tile:(t + 1) * tile])
            if t * tile >= GDN_CONV_COLS:
                z_ref[bb, :, t * tile - GDN_CONV_COLS:(t + 1) * tile - GDN_CONV_COLS] = proj
                continue
            for j in range(t * tile // LANES, (t + 1) * tile // LANES):
                cs = slice(j * LANES, (j + 1) * LANES)
                x = proj[:, j * LANES - t * tile:(j + 1) * LANES - t * tile]
                xx = jnp.concatenate([carry_ref[bb, :, cs], x], axis=0)
                base = CARRY_ROWS - (CONV_WIDTH - 1)
                acc = xx[base:base + ts] * cw_ref[0:1, cs]
                for k in range(1, CONV_WIDTH):
                    acc = acc + xx[base + k:base + k + ts] * cw_ref[k:k + 1, cs]
                carry_ref[bb, :, cs] = x[ts - CARRY_ROWS:ts]
                a = _silu(acc)
                if j < 2 * GDN_HEADS:
                    a = a * lax.rsqrt(jnp.sum(a * a, axis=-1, keepdims=True) + EPS)
                    if j < GDN_HEADS:
                        a = a * (GDN_DK ** -0.5)
                act_ref[bb, :, cs] = a

    causal, strict = _tri_masks()
    tril = _bf(jnp.where(causal, 1.0, 0.0))
    eye = jnp.where(causal & jnp.logical_not(strict), 1.0, 0.0).astype(F32)
    rrow = lax.broadcasted_iota(jnp.int32, (CHUNK, 4 * CHUNK), 0)
    rcol = lax.broadcasted_iota(jnp.int32, (CHUNK, 4 * CHUNK), 1)
    rhs_mask = ((rcol < CHUNK) & (rrow > rcol)) | (rcol >= 2 * CHUNK)
    hv = [slice(h * GDN_DV, (h + 1) * GDN_DV) for h in range(GDN_HEADS)]
    packed = GDN_HEADS * CHUNK
    prow = lax.broadcasted_iota(jnp.int32, (packed, packed), 0)
    pcol = lax.broadcasted_iota(jnp.int32, (packed, packed), 1)
    blk_mask = (prow // CHUNK) == (pcol // CHUNK)
    eye_p = jnp.concatenate([eye] * GDN_HEADS, axis=1)

    def solve_body(grp, carry):
        insts = [(bb, ci, h) for bb in range(nb) for ci in range(solve_group)
                 for h in range(GDN_HEADS)]
        n = len(insts)
        chunk_ids = [grp * solve_group + ci for ci in range(solve_group)]
        rows = [pl.ds(pl.multiple_of(c * CHUNK, CHUNK), CHUNK) for c in chunk_ids]
        beta_c = {(bb, ci): beta_ref[bb, rows[ci], :] for bb in range(nb)
                  for ci in range(solve_group)}
        g_c = {(bb, ci): g_ref[bb, rows[ci], :] for bb in range(nb) for ci in range(solve_group)}

        dg = [_dot_exact_lhs(tril, jnp.where(
            rhs_mask, g_c[bb, ci][:, SM_DECAY + h:SM_DECAY + h + 1], 0.0)) for bb, ci, h in insts]
        q = [act_ref[bb, rows[ci], h * GDN_DK:(h + 1) * GDN_DK] for bb, ci, h in insts]
        k = [act_ref[bb, rows[ci], GDN_QK + h * GDN_DK:GDN_QK + (h + 1) * GDN_DK]
             for bb, ci, h in insts]
        kb = [k[i] * beta_c[bb, ci][:, SM_BETA + h:SM_BETA + h + 1]
              for i, (bb, ci, h) in enumerate(insts)]
        kq = [_dot_nt(_bf(jnp.concatenate([kb[i], q[i]], axis=0)), _bf(k[i])) for i in range(n)]
        decay = [jnp.where(causal, jnp.exp(d[:, :CHUNK]), 0.0) for d in dg]
        gc = [d[:, 2 * CHUNK:] for d in dg]
        egc = [jnp.exp(x) for x in gc]
        a_mat = [jnp.where(strict, kq[i][:CHUNK] * decay[i], 0.0) for i in range(n)]
        groups = [(bb, ci) for bb in range(nb) for ci in range(solve_group)]
        ng = range(len(groups))
        a_p = [jnp.concatenate([a_mat[gi * GDN_HEADS + h] for h in range(GDN_HEADS)], axis=1)
               for gi in ng]

        def block_diag(x_bf):
            return jnp.where(blk_mask, jnp.concatenate([x_bf] * GDN_HEADS, axis=0),
                             jnp.zeros((), BF16))

        p = [_bf(x) for x in a_p]
        pd = [block_diag(x) for x in p]
        tinv = [eye_p - x for x in a_p]
        power = 2
        while True:
            pp = [_dot(p[gi], pd[gi]) for gi in ng]
            p = [_bf(x) for x in pp]
            pd = [block_diag(x) for x in p]
            xp = [_dot(_bf(tinv[gi]), pd[gi]) for gi in ng]
            tinv = [tinv[gi] + xp[gi] for gi in ng]
            power *= 2
            if power >= CHUNK:
                break
        for gi, (bb, ci) in enumerate(groups):
            tinv_ref[bb * n_chunks + chunk_ids[ci]] = _bf(tinv[gi])
        for i, (bb, ci, h) in enumerate(insts):
            idx = (bb * n_chunks + chunk_ids[ci]) * GDN_HEADS + h
            gl = gc[i][CHUNK - 1:CHUNK, :]
            kg = k[i] * egc[i]
            kg_hi = _bf(kg)
            kg_ref[idx, :CHUNK, :] = kg_hi
            kg_ref[idx, CHUNK:, :] = _bf(kg - kg_hi.astype(F32))
            qd_ref[bb, rows[ci], hv[h]] = _bf(q[i] * egc[i])
            kdt_ref[idx] = _bf((k[i] * jnp.exp(gl - gc[i])).T)
            qk_ref[bb * GDN_HEADS + h, rows[ci], :] = _bf(kq[i][CHUNK:] * decay[i])
            egl_ref[pl.ds(pl.multiple_of(idx * SUBLANES, SUBLANES), SUBLANES), :] = (
                jnp.broadcast_to(jnp.exp(gl), (SUBLANES, LANES)))
        return carry

    lax.fori_loop(0, n_chunks // solve_group, solve_body, 0)

    nw = nw_ref[...]
    chains = [(bb, h) for bb in range(nb) for h in range(GDN_HEADS)]
    nc = range(len(chains))
    states = [state_ref[bb * GDN_HEADS + h] for bb, h in chains]
    for c in range(n_chunks):
        rows = slice(c * CHUNK, (c + 1) * CHUNK)
        idx = [(bb * n_chunks + c) * GDN_HEADS + h for bb, h in chains]
        s_hi = [_bf(states[j]) for j in nc]
        s_lo = [_bf(states[j] - s_hi[j].astype(F32)) for j in nc]
        ks_a = [_dot(kg_ref[idx[j]], s_hi[j]) for j in nc]
        ks_b = [_dot(kg_ref[idx[j], :CHUNK, :], s_lo[j]) for j in nc]
        resid = [(act_ref[bb, rows, 2 * GDN_QK + h * GDN_DV:2 * GDN_QK + (h + 1) * GDN_DV]
                  - ((ks_b[j] + ks_a[j][CHUNK:]) + ks_a[j][:CHUNK]))
                 * beta_ref[bb, rows, SM_BETA + h:SM_BETA + h + 1]
                 for j, (bb, h) in enumerate(chains)]
        zero = jnp.zeros((CHUNK, GDN_DV), BF16)

        def block_diag_heads(xs):
            return jnp.concatenate(
                [jnp.concatenate([xs[h] if hh == h else zero for hh in range(GDN_HEADS)], axis=1)
                 for h in range(GDN_HEADS)], axis=0)

        v_new = []
        for bb in range(nb):
            r_bf = [_bf(resid[bb * GDN_HEADS + h]) for h in range(GDN_HEADS)]
            v_all = _dot(tinv_ref[bb * n_chunks + c], block_diag_heads(r_bf))
            v_new += [_bf(v_all[:, hv[h]]) for h in range(GDN_HEADS)]
        ds = [_dot(kdt_ref[idx[j]], v_new[j]) for j in nc]
        states = [states[j] * egl_ref[idx[j] * SUBLANES:idx[j] * SUBLANES + 1, :] + ds[j]
                  for j in nc]
        o = [_dot(qd_ref[bb, rows, hv[h]], s_hi[j])
             + _dot(qk_ref[bb * GDN_HEADS + h, rows, :], v_new[j])
             for j, (bb, h) in enumerate(chains)]
        for j, (bb, h) in enumerate(chains):
            o_ref[bb, rows, hv[h]] = _rms(o[j], nw) * _silu(z_ref[bb, rows, hv[h]])
    for j, (bb, h) in enumerate(chains):
        state_ref[bb * GDN_HEADS + h] = states[j]


def _gdn(x2, n_pre, wa, ws, cw, alog_row, dt_row, nw, layer, batch, seq,
         nb=2, ts=512, solve_group=8):
    ns = seq // ts
    n_chunks = ts // CHUNK
    n_inst = nb * n_chunks * GDN_HEADS
    x3 = x2.reshape(batch, seq, D_MODEL)
    out = pl.pallas_call(
        functools.partial(_gdn_kernel, nb=nb, ts=ts, solve_group=solve_group),
        grid=(batch // nb, ns),
        in_specs=[pl.BlockSpec((nb, ts, D_MODEL), lambda b, s: (b, s, 0)),
                  _const_spec((1, D_MODEL)),
                  _layer_spec((D_MODEL, GDN_COLS), layer),
                  _layer_spec((D_MODEL, LANES), layer),
                  _const_spec((CONV_WIDTH, GDN_CONV_COLS)),
                  _const_spec((1, LANES)),
                  _const_spec((1, LANES)),
                  _const_spec((1, GDN_DV))],
        out_specs=pl.BlockSpec((nb, ts, GDN_V), lambda b, s: (b, s, 0)),
        out_shape=jax.ShapeDtypeStruct((batch, seq, GDN_V), F32),
        scratch_shapes=[pltpu.VMEM((nb * GDN_HEADS, GDN_DK, GDN_DV), F32),
                        pltpu.VMEM((nb, CARRY_ROWS, GDN_CONV_COLS), F32),
                        pltpu.VMEM((nb, ts, D_MODEL), BF16),
                        pltpu.VMEM((nb, ts, GDN_CONV_COLS), F32),
                        pltpu.VMEM((nb, ts, GDN_V), F32),
                        pltpu.VMEM((nb, ts, LANES), F32),
                        pltpu.VMEM((nb, ts, LANES), F32),
                        pltpu.VMEM((nb * n_chunks, CHUNK, GDN_HEADS * CHUNK), BF16),
                        pltpu.VMEM((n_inst, 2 * CHUNK, GDN_DK), BF16),
                        pltpu.VMEM((nb, ts, GDN_QK), BF16),
                        pltpu.VMEM((n_inst, GDN_DK, CHUNK), BF16),
                        pltpu.VMEM((nb * GDN_HEADS, ts, CHUNK), BF16),
                        pltpu.VMEM((n_inst * SUBLANES, LANES), F32)],
        compiler_params=pltpu.CompilerParams(
            dimension_semantics=("arbitrary", "arbitrary"), vmem_limit_bytes=VMEM_LIMIT),
        name="gdn_mixer",
    )(x3, n_pre, wa, ws, cw, alog_row, dt_row, nw)
    return out.reshape(batch * seq, GDN_V)


PB_Q, PB_K, PB_V, PB_R = 0, GLA_QK, 2 * GLA_QK, 2 * GLA_QK + GLA_V


def _gla_kernel(x_ref, npre_ref, wb_ref, ws_ref, w2_ref, gb_ref, nw_ref, o_ref,
                state_ref, h_ref, pb_ref, b_ref, *, ts):
    n_chunks = ts // CHUNK

    @pl.when(pl.program_id(1) == 0)
    def _():
        state_ref[...] = jnp.zeros_like(state_ref)

    h_ref[...] = _bf(_rms(x_ref[...], npre_ref[...]))

    def project(col, width):
        pb_ref[:, col:col + width] = _dot(h_ref[...], wb_ref[:, col:col + width])

    sm = _dot(h_ref[...], ws_ref[...])
    logit = _dot_x3(sm, w2_ref[...]) + gb_ref[...]
    log_a = -_softplus(-logit) * (1.0 / GLA_TAU)
    causal, _ = _tri_masks()
    tril = _bf(jnp.where(causal, 1.0, 0.0))
    rows = [slice(c * CHUNK, (c + 1) * CHUNK) for c in range(n_chunks)]
    for r in rows:
        b_ref[r, :] = _dot_exact_lhs(tril, log_a[r, :])

    project(PB_Q, 2 * GLA_QK)
    project(PB_V, GLA_DV)
    project(PB_R, GLA_DV)
    nw = nw_ref[...]
    cs = range(n_chunks)
    for h in range(GLA_HEADS):
        kc = slice(h * GLA_DK, (h + 1) * GLA_DK)
        vcol = PB_V + h * GLA_DV
        rcol = PB_R + h * GLA_DV
        b = [b_ref[r, kc] for r in rows]
        q = [pb_ref[r, PB_Q + h * GLA_DK:PB_Q + (h + 1) * GLA_DK] * (GLA_DK ** -0.5) for r in rows]
        k = [pb_ref[r, PB_K + h * GLA_DK:PB_K + (h + 1) * GLA_DK] for r in rows]
        v = [pb_ref[r, vcol:vcol + GLA_DV] for r in rows]
        b_last = [x[CHUNK - 1:CHUNK, :] for x in b]
        b_mid = [x[CHUNK // 2:CHUNK // 2 + 1, :] for x in b]
        attn = [_dot_nt(_bf(q[c] * jnp.exp(b[c] - b_mid[c])), _bf(k[c] * jnp.exp(b_mid[c] - b[c])))
                for c in cs]
        if h + 1 < GLA_HEADS:
            project(vcol + GLA_DV, GLA_DV)
        d_state =[_dot(_bf((k[c] * jnp.exp(b_last[c] - b[c])).T), _bf(v[c])) for c in cs]
        decay_col = [jnp.exp(b[c].T[:, CHUNK - 1:CHUNK]) for c in cs]
        o_intra = [_dot(_bf(jnp.where(causal, attn[c], 0.0)), _bf(v[c])) for c in cs]
        if h + 1 < GLA_HEADS:
            project(rcol + GLA_DV, GLA_DV)
        state = state_ref[h]
        o_inter = []
        for c in cs:
            o_inter.append(_dot(_bf(q[c] * jnp.exp(b[c])), _bf(state)))
            state = state * decay_col[c] + d_state[c]
        state_ref[h] = state
        for c in cs:
            o_ref[rows[c], h * GLA_DV:(h + 1) * GLA_DV] = (
                _rms(o_intra[c] + o_inter[c], nw) * _silu(pb_ref[rows[c], rcol:rcol + GLA_DV]))


def _gla(x2, n_pre, wb, ws, w2pad, gate_b, nw, layer, batch, seq, ts=1024):
    ns = seq // ts
    row = lambda b, s: (b * ns + s, 0)
    return pl.pallas_call(
        functools.partial(_gla_kernel, ts=ts),
        grid=(batch, ns),
        in_specs=[pl.BlockSpec((ts, D_MODEL), row),
                  _const_spec((1, D_MODEL)),
                  _layer_spec((D_MODEL, GLA_COLS), layer),
                  _layer_spec((D_MODEL, LANES), layer),
                  _const_spec((LANES, GLA_QK)),
                  _const_spec((1, GLA_QK)),
                  _const_spec((1, GLA_DV))],
        out_specs=pl.BlockSpec((ts, GLA_V), row),
        out_shape=jax.ShapeDtypeStruct((batch * seq, GLA_V), F32),
        scratch_shapes=[pltpu.VMEM((GLA_HEADS, GLA_DK, GLA_DV), F32),
                        pltpu.VMEM((ts, D_MODEL), BF16),
                        pltpu.VMEM((ts, GLA_COLS), F32),
                        pltpu.VMEM((ts, GLA_QK), F32)],
        compiler_params=pltpu.CompilerParams(
            dimension_semantics=("arbitrary", "arbitrary"), vmem_limit_bytes=VMEM_LIMIT),
        name="gla_mixer",
    )(x2, n_pre, wb, ws, w2pad, gate_b, nw)


def _merge_mlp_kernel(ya_ref, yb_ref, x_ref, n_mix_ref, wg_ref, woa_ref, wob_ref, wo_ref,
                      n_post_ref, n_pre_ref, wup_ref, wdn_ref, n_mlp_ref, o_ref):
    x = x_ref[...]
    hg = _bf(_rms(x, n_mix_ref[...]))
    ya = _dot(_bf(ya_ref[...]), woa_ref[...])
    merged = _sigmoid(_dot(hg, wg_ref[:, :D_MODEL])) * ya
    yb = _dot(_bf(yb_ref[...]), wob_ref[...])
    merged = merged + _sigmoid(_dot(hg, wg_ref[:, D_MODEL:])) * yb
    y = _dot(_bf(merged), wo_ref[...])
    x1 = x + _rms(y, n_post_ref[...])
    h = _bf(_rms(x1, n_pre_ref[...]))
    acc = jnp.zeros(x1.shape, F32)
    for c in range(D_FF // D_MODEL):
        cs = slice(c * D_MODEL, (c + 1) * D_MODEL)
        u = jnp.square(jnp.maximum(_dot(h, wup_ref[:, cs]), 0.0))
        acc = acc + _dot(_bf(u), wdn_ref[cs, :])
    o_ref[...] = x1 + _rms(acc, n_mlp_ref[...])


def _merge_mlp(ya, yb, x2, n_mix, wg, woa, wob, wo, n_post, n_pre, wup, wdn, n_mlp, layer, tm=512):
    t = x2.shape[0]
    row = lambda i: (i, 0)
    return pl.pallas_call(
        _merge_mlp_kernel,
        grid=(t // tm,),
        in_specs=[pl.BlockSpec((tm, GDN_V), row),
                  pl.BlockSpec((tm, GLA_V), row),
                  pl.BlockSpec((tm, D_MODEL), row),
                  _const_spec((1, D_MODEL)),
                  _layer_spec((D_MODEL, GATE_COLS), layer),
                  _layer_spec((GDN_V, D_MODEL), layer),
                  _layer_spec((GLA_V, D_MODEL), layer),
                  _layer_spec((D_MODEL, D_MODEL), layer),
                  _const_spec((1, D_MODEL)),
                  _const_spec((1, D_MODEL)),
                  _layer_spec((D_MODEL, D_FF), layer),
                  _layer_spec((D_FF, D_MODEL), layer),
                  _const_spec((1, D_MODEL))],
        out_specs=pl.BlockSpec((tm, D_MODEL), row),
        out_shape=jax.ShapeDtypeStruct((t, D_MODEL), F32),
        compiler_params=pltpu.CompilerParams(
            dimension_semantics=("arbitrary",), vmem_limit_bytes=VMEM_LIMIT),
        name="merge_mlp",
    )(ya, yb, x2, n_mix, wg, woa, wob, wo, n_post, n_pre, wup, wdn, n_mlp)


def _pad_row(v, offset):
    return jnp.zeros((1, LANES), F32).at[0, offset:offset + v.shape[0]].set(v)


def kernel(x, w_in, conv_w, a_log, dt_bias, gdn_norm, gla_gate_w2, gla_gate_b, gla_norm,
           w_out_a, w_out_b, w_o, norm_mix_pre, norm_mix_post, norm_mlp_pre, norm_mlp_post,
           w_mlp_up, w_mlp_down):
    batch, seq, d = x.shape
    x2 = x.reshape(batch * seq, d)
    w_a = w_in[:, :, OFF_GDN:OFF_GDN + GDN_COLS].astype(BF16)
    w_b = w_in[:, :, OFF_GLA:OFF_GLA + GLA_COLS].astype(BF16)
    w_g = w_in[:, :, OFF_GATE:OFF_GATE + GATE_COLS].astype(BF16)
    n_small = 2 * GDN_HEADS + GLA_GATE_RANK
    w_s = jnp.concatenate(
        [w_in[:, :, OFF_BETA:OFF_BETA + 2 * GDN_HEADS], w_in[:, :, OFF_GLR:OFF_GLR + GLA_GATE_RANK],
         jnp.zeros((DEPTH, d, LANES - n_small), w_in.dtype)], axis=-1).astype(BF16)
    woa_b, wob_b, wo_b = w_out_a.astype(BF16), w_out_b.astype(BF16), w_o.astype(BF16)
    wup_b, wdn_b = w_mlp_up.astype(BF16), w_mlp_down.astype(BF16)
    w2pad = jnp.zeros((DEPTH, LANES, GLA_QK), F32).at[:, SM_GLR:SM_GLR + GLA_GATE_RANK].set(
        gla_gate_w2)
    for l in range(DEPTH):
        n_mix = norm_mix_pre[l][None, :]
        ya = _gdn(x2, n_mix, w_a, w_s, conv_w[l], _pad_row(a_log[l], SM_DECAY),
                  _pad_row(dt_bias[l], SM_DECAY), gdn_norm[l][None, :], l, batch, seq)
        yb = _gla(x2, n_mix, w_b, w_s, w2pad[l], gla_gate_b[l][None, :], gla_norm[l][None, :],
                  l, batch, seq)
        x2 = _merge_mlp(ya, yb, x2, n_mix, w_g, woa_b, wob_b, wo_b,
                        norm_mix_post[l][None, :], norm_mlp_pre[l][None, :],
                        wup_b, wdn_b, norm_mlp_post[l][None, :], l)
    return x2.reshape(batch, seq, d)
```
